```python
import jax, jax.numpy as jnp
from jax import lax
import numpy as np

D_MODEL = 4096
BATCH = 1
SEQ = 8192
DEPTH = 2
DEC_BATCH = 8
DEC_SEQ = 2048
PAST_LEN = 128

ROPE_THETA = 500000.0
LN_EPS = 1e-5
RMS_EPS = 1e-6
NEG_BIG = -1e30

DIL_GROUPS = ((128, 1), (512, 4), (2048, 16))
A_HEADS_PER_GROUP = 8
A_HEADS = A_HEADS_PER_GROUP * len(DIL_GROUPS)
A_HEAD_DIM = 128
A_ROT_DIM = A_HEAD_DIM // 4
A_WIDTH = A_HEADS * A_HEAD_DIM

B_HEADS = 64
Q_LORA = 1024
KV_LORA = 512
NOPE_DIM = 128
ROPE_DIM = 64
V_DIM = 128
QK_DIM = NOPE_DIM + ROPE_DIM
B_WIDTH = B_HEADS * V_DIM
Q_BLOCK = 128

FF_DIM = ((8 * D_MODEL + 3 * 256 - 1) // (3 * 256)) * 256

ALPHA = (2.0 * DEPTH) ** 0.25
BETA = (8.0 * DEPTH) ** -0.25

OFF_QA = 0
OFF_KA = OFF_QA + A_WIDTH
OFF_VA = OFF_KA + A_WIDTH
OFF_CQ = OFF_VA + A_WIDTH
OFF_CKV = OFF_CQ + Q_LORA
OFF_KR = OFF_CKV + KV_LORA
OFF_GA = OFF_KR + ROPE_DIM
OFF_GB = OFF_GA + D_MODEL
N_IN = OFF_GB + D_MODEL

kernel_name = 'hybrid_dilated_mla_encoder'


def layer_norm(x, g, b):
    xf = x.astype(jnp.float32)
    mu = jnp.mean(xf, axis=-1, keepdims=True)
    xc = xf - mu
    var = jnp.mean(xc * xc, axis=-1, keepdims=True)
    return (xc * lax.rsqrt(var + LN_EPS) * g + b).astype(x.dtype)


def rms_norm(x, g):
    xf = x.astype(jnp.float32)
    y = xf * lax.rsqrt(jnp.mean(xf * xf, axis=-1, keepdims=True) + RMS_EPS)
    return (y * g).astype(x.dtype)


def rope(x, pos, rot_dim):
    half = rot_dim // 2
    inv_freq = ROPE_THETA ** (-jnp.arange(half, dtype=jnp.float32) / half)
    ang = pos[:, None] * inv_freq[None, :]
    cos = jnp.cos(ang)[None, :, None, :].astype(x.dtype)
    sin = jnp.sin(ang)[None, :, None, :].astype(x.dtype)
    x1 = x[..., :half]
    x2 = x[..., half:rot_dim]
    return jnp.concatenate([x1 * cos - x2 * sin, x1 * sin + x2 * cos, x[..., rot_dim:]], axis=-1)


def banded_attention(q, k, v, w):
    bq, n, h, dh = q.shape
    nb = -(-n // w)
    n_pad = nb * w
    qb = jnp.pad(q, ((0, 0), (0, n_pad - n), (0, 0), (0, 0))).reshape(bq, nb, w, h, dh)
    pad_kv = ((0, 0), (w, n_pad - n + w), (0, 0), (0, 0))
    kp = jnp.pad(k, pad_kv).reshape(bq, nb + 2, w, h, dh)
    vp = jnp.pad(v, pad_kv).reshape(bq, nb + 2, w, h, dh)
    kb = jnp.concatenate([kp[:, :-2], kp[:, 1:-1], kp[:, 2:]], axis=2)
    vb = jnp.concatenate([vp[:, :-2], vp[:, 1:-1], vp[:, 2:]], axis=2)
    qpos = jnp.arange(nb)[:, None] * w + jnp.arange(w)[None, :]
    kpos = jnp.arange(nb)[:, None] * w - w + jnp.arange(3 * w)[None, :]
    rel = kpos[:, None, :] - qpos[:, :, None]
    valid = (jnp.abs(rel) <= w) & (kpos[:, None, :] >= 0) & (kpos[:, None, :] < n)
    s = jnp.einsum('bnqhd,bnkhd->bnhqk', qb, kb).astype(jnp.float32) * (dh ** -0.5)
    s = jnp.where(valid[None, :, None], s, NEG_BIG)
    lse = jax.nn.logsumexp(s, axis=-1)
    p = jnp.exp(s - lse[..., None])
    out = jnp.einsum('bnhqk,bnkhd->bnqhd', p.astype(vb.dtype), vb).reshape(bq, n_pad, h, dh)[:, :n]
    lse = lse.transpose(0, 1, 3, 2).reshape(bq, n_pad, h)[:, :n]
    return out, lse


def dilated_attention(q, k, v, window, dilation):
    bsz, seq, h, dh = q.shape
    n = seq // dilation

    def to_res(t):
        return t.reshape(bsz, n, dilation, h, dh).transpose(0, 2, 1, 3, 4).reshape(bsz * dilation, n, h, dh)

    out, lse = banded_attention(to_res(q), to_res(k), to_res(v), window // (2 * dilation))
    out = out.reshape(bsz, dilation, n, h, dh).transpose(0, 2, 1, 3, 4).reshape(bsz, seq, h, dh)
    lse = lse.reshape(bsz, dilation, n, h).transpose(0, 2, 1, 3).reshape(bsz, seq, h)
    return out, lse


def dense_attention(q, k, v, scale):
    bsz, seq, h, dq = q.shape
    nq = seq // Q_BLOCK
    qb = q.reshape(bsz, nq, Q_BLOCK, h, dq).transpose(1, 0, 2, 3, 4)

    def one_block(qblk):
        s = jnp.einsum('bqhd,bkhd->bhqk', qblk, k).astype(jnp.float32) * scale
        p = jax.nn.softmax(s, axis=-1)
        return jnp.einsum('bhqk,bkhd->bqhd', p.astype(v.dtype), v)

    out = lax.map(one_block, qb)
    return out.transpose(1, 0, 2, 3, 4).reshape(bsz, seq, h, v.shape[-1])


def encoder_layer(x, w_in, q_norm, w_uq, kv_norm, w_ukv, w_ba, w_bb, w_o,
                  ln1_g, ln1_b, w_gate, w_up, w_down, ln2_g, ln2_b):
    bsz, seq, _ = x.shape
    pos = jnp.arange(seq, dtype=jnp.float32)
    proj = x @ w_in

    qa = rope(proj[..., OFF_QA:OFF_KA].reshape(bsz, seq, A_HEADS, A_HEAD_DIM), pos, A_ROT_DIM)
    ka = rope(proj[..., OFF_KA:OFF_VA].reshape(bsz, seq, A_HEADS, A_HEAD_DIM), pos, A_ROT_DIM)
    va = proj[..., OFF_VA:OFF_CQ].reshape(bsz, seq, A_HEADS, A_HEAD_DIM)
    outs, lses = [], []
    for g, (win, dil) in enumerate(DIL_GROUPS):
        sl = slice(g * A_HEADS_PER_GROUP, (g + 1) * A_HEADS_PER_GROUP)
        o, s = dilated_attention(qa[:, :, sl], ka[:, :, sl], va[:, :, sl], win, dil)
        outs.append(o)
        lses.append(s)
    mix_w = jax.nn.softmax(jnp.stack(lses, axis=0), axis=0)
    out_a = jnp.concatenate(
        [outs[g] * mix_w[g][..., None].astype(outs[g].dtype) for g in range(len(DIL_GROUPS))],
        axis=2).reshape(bsz, seq, A_WIDTH)

    c_q = rms_norm(proj[..., OFF_CQ:OFF_CKV], q_norm)
    q = (c_q @ w_uq).reshape(bsz, seq, B_HEADS, QK_DIM)
    q = jnp.concatenate([q[..., :NOPE_DIM], rope(q[..., NOPE_DIM:], pos, ROPE_DIM)], axis=-1)
    c_kv = rms_norm(proj[..., OFF_CKV:OFF_KR], kv_norm)
    kv = (c_kv @ w_ukv).reshape(bsz, seq, B_HEADS, NOPE_DIM + V_DIM)
    k_pe = rope(proj[..., OFF_KR:OFF_GA][:, :, None, :], pos, ROPE_DIM)
    k = jnp.concatenate([kv[..., :NOPE_DIM],
                         jnp.broadcast_to(k_pe, (bsz, seq, B_HEADS, ROPE_DIM))], axis=-1)
    out_b = dense_attention(q, k, kv[..., NOPE_DIM:], QK_DIM ** -0.5).reshape(bsz, seq, B_WIDTH)

    gate_a = jax.nn.sigmoid(proj[..., OFF_GA:OFF_GB])
    gate_b = jax.nn.sigmoid(proj[..., OFF_GB:N_IN])
    merged = gate_a * (out_a @ w_ba) + gate_b * (out_b @ w_bb)
    x = layer_norm(ALPHA * x + merged @ w_o, ln1_g, ln1_b)

    ffn = (jax.nn.silu(x @ w_gate) * (x @ w_up)) @ w_down
    return layer_norm(ALPHA * x + ffn, ln2_g, ln2_b)


def setup_inputs(seed: int = 0) -> dict:
    key = jax.random.key(seed)
    ks = jax.random.split(key, 17)

    def nrm(k, shape, scale):
        return jax.random.normal(k, shape, jnp.float32) * scale

    return {
        'x_prompt': nrm(ks[0], (BATCH, SEQ, D_MODEL), 1.0),
        'x_sample': nrm(ks[1], (DEC_BATCH, DEC_SEQ, D_MODEL), 1.0),
        'w_in': nrm(ks[2], (DEPTH, D_MODEL, N_IN), D_MODEL ** -0.5),
        'mla_q_norm': 1.0 + nrm(ks[3], (DEPTH, Q_LORA), 0.02),
        'w_uq': nrm(ks[4], (DEPTH, Q_LORA, B_HEADS * QK_DIM), Q_LORA ** -0.5),
        'mla_kv_norm': 1.0 + nrm(ks[5], (DEPTH, KV_LORA), 0.02),
        'w_ukv': nrm(ks[6], (DEPTH, KV_LORA, B_HEADS * (NOPE_DIM + V_DIM)), KV_LORA ** -0.5),
        'w_branch_a': nrm(ks[7], (DEPTH, A_WIDTH, D_MODEL), BETA * A_WIDTH ** -0.5),
        'w_branch_b': nrm(ks[8], (DEPTH, B_WIDTH, D_MODEL), BETA * B_WIDTH ** -0.5),
        'w_out': nrm(ks[9], (DEPTH, D_MODEL, D_MODEL), BETA * D_MODEL ** -0.5),
        'ln1_g': 1.0 + nrm(ks[10], (DEPTH, D_MODEL), 0.02),
        'ln1_b': nrm(ks[11], (DEPTH, D_MODEL), 0.02),
        'w_ffn_gate': nrm(ks[12], (DEPTH, D_MODEL, FF_DIM), D_MODEL ** -0.5),
        'w_ffn_up': nrm(ks[13], (DEPTH, D_MODEL, FF_DIM), D_MODEL ** -0.5),
        'w_ffn_down': nrm(ks[14], (DEPTH, FF_DIM, D_MODEL), BETA * FF_DIM ** -0.5),
        'ln2_g': 1.0 + nrm(ks[15], (DEPTH, D_MODEL), 0.02),
        'ln2_b': nrm(ks[16], (DEPTH, D_MODEL), 0.02),
    }


def reference(x_prompt, x_sample, w_in, mla_q_norm, w_uq, mla_kv_norm, w_ukv, w_branch_a,
              w_branch_b, w_out, ln1_g, ln1_b, w_ffn_gate, w_ffn_up, w_ffn_down, ln2_g, ln2_b):
    params = (w_in, mla_q_norm, w_uq, mla_kv_norm, w_ukv, w_branch_a, w_branch_b, w_out,
              ln1_g, ln1_b, w_ffn_gate, w_ffn_up, w_ffn_down, ln2_g, ln2_b)

    def trunk(x):
        for layer in range(DEPTH):
            x = encoder_layer(x, *[p[layer] for p in params])
        return x

    y_prompt = trunk(x_prompt)
    y_sample = trunk(x_sample)
    return (y_prompt, y_sample)
```

```python
import functools
from typing import NamedTuple

import jax
import jax.numpy as jnp
from jax import lax
from jax.experimental import pallas as pl
from jax.experimental.pallas import tpu as pltpu

F32 = jnp.float32
BF16 = jnp.bfloat16

LANES = 128
ROPE_THETA = 500000.0
LN_EPS = 1e-5
RMS_EPS = 1e-6
NEG_BIG = -1e30

DIL_GROUPS = ((128, 1), (512, 4), (2048, 16))
BAND_W = 64
A_HEAD_DIM = 128
A_ROT_DIM = 32
NOPE_DIM = 128
ROPE_DIM = 64
V_DIM = 128
QK_DIM = NOPE_DIM + ROPE_DIM
Q_HEAD_PAD = 256
VMEM_LIMIT_MB = 56


class Cfg(NamedTuple):
    d_model: int
    prompt_len: int
    sample_len: int
    n_sample: int
    depth: int
    hg: int
    b_heads: int
    q_lora: int
    kv_lora: int
    ff: int

    @property
    def m(self):
        return self.prompt_len + self.n_sample * self.sample_len

    @property
    def a_width(self):
        return len(DIL_GROUPS) * self.hg * A_HEAD_DIM

    @property
    def ff_pad(self):
        return -(-self.ff // 1024) * 1024 if self.ff > 1024 else -(-self.ff // LANES) * LANES

    @property
    def alpha(self):
        return (2.0 * self.depth) ** 0.25


def _tile(n, pref):
    if n <= pref:
        return n
    t = (pref // LANES) * LANES
    while n % t:
        t -= LANES
    return t


def _params(sem):
    return pltpu.CompilerParams(dimension_semantics=sem,
                                vmem_limit_bytes=VMEM_LIMIT_MB * 1024 * 1024)


def _matmul(a, bs, extras, out_defs, epilogue, *, tm, tn, tk=None, name):
    m, k_dim = a.shape
    n = bs[0].shape[1]
    tk = k_dim if tk is None else tk
    nk = k_dim // tk
    nb, ne, no = len(bs), len(extras), len(out_defs)

    if nk == 1:
        grid = (m // tm, n // tn)
        wrap = lambda f: f
        a_spec = pl.BlockSpec((tm, tk), lambda i, j: (i, 0))
        b_spec = pl.BlockSpec((tk, tn), lambda i, j: (0, j))
        sem = ("parallel", "arbitrary")
    else:
        grid = (m // tm, n // tn, nk)
        wrap = lambda f: (lambda i, j, k: f(i, j))
        a_spec = pl.BlockSpec((tm, tk), lambda i, j, k: (i, k))
        b_spec = pl.BlockSpec((tk, tn), lambda i, j, k: (k, j))
        sem = ("parallel", "arbitrary", "arbitrary")

    in_specs = [a_spec] + [b_spec] * nb + [pl.BlockSpec(bshape, wrap(f)) for _, bshape, f in extras]
    out_specs = [pl.BlockSpec((tm, bc), wrap(lambda i, j: (i, j))) for _, _, bc in out_defs]
    out_shape = [jax.ShapeDtypeStruct((m, nc), dt) for nc, dt, _ in out_defs]
    scratch = [pltpu.VMEM((tm, tn), F32) for _ in range(nb)] if nk > 1 else []

    def kernel(*refs):
        a_ref = refs[0]
        b_refs = refs[1:1 + nb]
        e_refs = refs[1 + nb:1 + nb + ne]
        o_refs = refs[1 + nb + ne:1 + nb + ne + no]
        acc_refs = refs[1 + nb + ne + no:]
        j = pl.program_id(1)
        if nk == 1:
            accs = [jnp.dot(a_ref[...], b[...], preferred_element_type=F32) for b in b_refs]
            epilogue(accs, e_refs, o_refs, j)
        else:
            k = pl.program_id(2)

            @pl.when(k == 0)
            def _():
                for b, acc in zip(b_refs, acc_refs):
                    acc[...] = jnp.dot(a_ref[...], b[...], preferred_element_type=F32)

            @pl.when(k > 0)
            def _():
                for b, acc in zip(b_refs, acc_refs):
                    acc[...] += jnp.dot(a_ref[...], b[...], preferred_element_type=F32)

            @pl.when(k == nk - 1)
            def _():
                epilogue([acc[...] for acc in acc_refs], e_refs, o_refs, j)

    outs = pl.pallas_call(
        kernel, grid=grid, in_specs=in_specs, out_specs=out_specs, out_shape=out_shape,
        scratch_shapes=scratch, compiler_params=_params(sem), name=name,
    )(a, *bs, *[e[0] for e in extras])
    return outs


def _rope_lanes(x, c, sl, sr, half):
    return x * c + pltpu.roll(x, LANES - half, 1) * sl + pltpu.roll(x, half, 1) * sr


def _row_tables(tabs, tm):
    return [(t, (tm, LANES), lambda i, j: (i, 0)) for t in tabs]


def _proj_qkv(x_bf, w_qkv, tabs_a, cfg):
    tm, tn = _tile(cfg.m, 1024), _tile(cfg.a_width, 512)
    n_rope_blocks = 2 * cfg.a_width // tn
    half = A_ROT_DIM // 2

    def epilogue(accs, e, o, j):
        acc = accs[0]

        @pl.when(j < n_rope_blocks)
        def _():
            c, sl, sr = e[0][...], e[1][...], e[2][...]
            for h in range(tn // LANES):
                hs = slice(h * LANES, (h + 1) * LANES)
                o[0][:, hs] = _rope_lanes(acc[:, hs], c, sl, sr, half).astype(BF16)

        @pl.when(j >= n_rope_blocks)
        def _():
            o[0][...] = acc.astype(BF16)

    return _matmul(x_bf, [w_qkv], _row_tables(tabs_a, tm), [(3 * cfg.a_width, BF16, tn)], epilogue,
                   tm=tm, tn=tn, name="proj_qkv")[0]


def _proj_latent(x_bf, w_lat, q_norm, kv_norm, tabs_b, cfg):
    ql, kl = cfg.q_lora, cfg.kv_lora
    n = ql + kl + LANES
    tm = _tile(cfg.m, 512)
    half = ROPE_DIM // 2

    def rms(v, g):
        return v * lax.rsqrt(jnp.mean(v * v, axis=-1, keepdims=True) + RMS_EPS) * g

    def epilogue(accs, e, o, j):
        acc = accs[0]
        o[0][...] = rms(acc[:, :ql], e[0][...]).astype(BF16)
        o[1][...] = rms(acc[:, ql:ql + kl], e[1][...]).astype(BF16)
        o[2][...] = _rope_lanes(acc[:, ql + kl:], e[2][...], e[3][...], e[4][...], half).astype(BF16)

    extras = [(q_norm.reshape(1, ql), (1, ql), lambda i, j: (0, 0)),
              (kv_norm.reshape(1, kl), (1, kl), lambda i, j: (0, 0))] + _row_tables(tabs_b, tm)
    return _matmul(x_bf, [w_lat], extras, [(ql, BF16, ql), (kl, BF16, kl), (LANES, BF16, LANES)],
                   epilogue, tm=tm, tn=n, name="proj_latent")


def _proj_gates(x_bf, w_g, cfg):
    tm, tn = _tile(cfg.m, 1024), _tile(2 * cfg.d_model, 512)

    def epilogue(accs, e, o, j):
        o[0][...] = jax.nn.sigmoid(accs[0])

    return _matmul(x_bf, [w_g], [], [(2 * cfg.d_model, F32, tn)], epilogue, tm=tm, tn=tn,
                   name="proj_gates")[0]


def _proj_q(c_q, w_uq_pad, tabs_b, cfg):
    tm, tn = _tile(cfg.m, 1024), _tile(cfg.b_heads * Q_HEAD_PAD, 1024)
    half = ROPE_DIM // 2
    scale = QK_DIM ** -0.5

    def epilogue(accs, e, o, j):
        acc = accs[0]
        c, sl, sr = e[0][...], e[1][...], e[2][...]
        for h in range(tn // Q_HEAD_PAD):
            lo = h * Q_HEAD_PAD
            o[0][:, lo:lo + LANES] = (acc[:, lo:lo + LANES] * scale).astype(BF16)
            pe = _rope_lanes(acc[:, lo + LANES:lo + 2 * LANES], c, sl, sr, half)
            o[0][:, lo + LANES:lo + 2 * LANES] = (pe * scale).astype(BF16)

    return _matmul(c_q, [w_uq_pad], _row_tables(tabs_b, tm), [(cfg.b_heads * Q_HEAD_PAD, BF16, tn)],
                   epilogue, tm=tm, tn=tn, name="proj_q")[0]


def _proj_kv(c_kv, w_ukv, cfg):
    tm, tn = _tile(cfg.m, 1024), _tile(cfg.b_heads * 256, 1024)

    def epilogue(accs, e, o, j):
        o[0][...] = accs[0].astype(BF16)

    return _matmul(c_kv, [w_ukv], [], [(cfg.b_heads * 256, BF16, tn)], epilogue, tm=tm, tn=tn,
                   name="proj_kv")[0]


def _branch_a(out_a, w_ba, gates, cfg):
    tm, tn = _tile(cfg.m, 1024), _tile(cfg.d_model, 512)

    def epilogue(accs, e, o, j):
        o[0][...] = e[0][...] * accs[0]

    extras = [(gates, (tm, tn), lambda i, j: (i, j))]
    return _matmul(out_a, [w_ba], extras, [(cfg.d_model, F32, tn)], epilogue, tm=tm, tn=tn,
                   name="branch_a")[0]


def _branch_b_merge(out_b, w_bb, gates, part_a, cfg):
    tm, tn = _tile(cfg.m, 1024), _tile(cfg.d_model, 512)
    tk = _tile(out_b.shape[1], 2048)
    goff = cfg.d_model // tn

    def epilogue(accs, e, o, j):
        o[0][...] = (e[1][...] + e[0][...] * accs[0]).astype(BF16)

    extras = [(gates, (tm, tn), lambda i, j: (i, j + goff)),
              (part_a, (tm, tn), lambda i, j: (i, j))]
    return _matmul(out_b, [w_bb], extras, [(cfg.d_model, BF16, tn)], epilogue, tm=tm, tn=tn, tk=tk,
                   name="branch_b_merge")[0]


def _residual_matmul(a, w, x_res, cfg, *, tk, name):
    tm, tn = _tile(cfg.m, 1024), _tile(cfg.d_model, 512)
    alpha = cfg.alpha

    def epilogue(accs, e, o, j):
        o[0][...] = alpha * e[0][...] + accs[0]

    extras = [(x_res, (tm, tn), lambda i, j: (i, j))]
    return _matmul(a, [w], extras, [(cfg.d_model, F32, tn)], epilogue, tm=tm, tn=tn, tk=tk,
                   name=name)[0]


def _swiglu(x_bf, w_gate, w_up, cfg):
    tm, tn = _tile(cfg.m, 1024), _tile(cfg.ff_pad, 512)

    def epilogue(accs, e, o, j):
        g, u = accs
        o[0][...] = (g * jax.nn.sigmoid(g) * u).astype(BF16)

    return _matmul(x_bf, [w_gate, w_up], [], [(cfg.ff_pad, BF16, tn)], epilogue, tm=tm, tn=tn,
                   name="swiglu")[0]


def _layer_norm(y, g, b, cfg):
    m, d = y.shape
    tm = _tile(m, 256)

    def kernel(y_ref, g_ref, b_ref, o_ref, obf_ref):
        v = y_ref[...]
        mu = jnp.mean(v, axis=-1, keepdims=True)
        vc = v - mu
        var = jnp.mean(vc * vc, axis=-1, keepdims=True)
        out = vc * lax.rsqrt(var + LN_EPS) * g_ref[...] + b_ref[...]
        o_ref[...] = out
        obf_ref[...] = out.astype(BF16)

    row = pl.BlockSpec((tm, d), lambda i: (i, 0))
    vec = pl.BlockSpec((1, d), lambda i: (0, 0))
    return pl.pallas_call(
        kernel, grid=(m // tm,), in_specs=[row, vec, vec], out_specs=[row, row],
        out_shape=[jax.ShapeDtypeStruct((m, d), F32), jax.ShapeDtypeStruct((m, d), BF16)],
        compiler_params=_params(("parallel",)), name="layer_norm",
    )(y, g.reshape(1, d), b.reshape(1, d))


def _segment(row0, p_len, s_len):
    in_prompt = row0 < p_len
    s_idx = jnp.maximum(row0 - p_len, 0) // s_len
    lo = jnp.where(in_prompt, 0, p_len + s_idx * s_len)
    return lo, jnp.where(in_prompt, p_len, s_len)


def _banded_attention(qkv, g, dil, cfg):
    m = cfg.m
    hg = cfg.hg
    aw = cfg.a_width
    rows = m // dil
    p_len, s_len = cfg.prompt_len // dil, cfg.sample_len // dil
    bq = 2 * BAND_W
    kw = bq + 2 * BAND_W
    rc = _tile(rows, 2048)
    nsub = rc // bq
    blocks_per_row = 3 * aw // LANES
    scale = A_HEAD_DIM ** -0.5
    view = qkv.reshape(rows, dil * 3 * aw)

    def kernel(q_ref, k_ref, v_ref, o_ref, lse_ref):
        base = pl.program_id(2) * rc

        def body(t, carry):
            s0 = pl.multiple_of(t * bq, bq)
            g0 = base + s0
            ks = pl.multiple_of(jnp.clip(g0 - BAND_W, 0, rows - kw), BAND_W)
            q = q_ref[pl.ds(s0, bq), :]
            k = k_ref[pl.ds(ks, kw), :]
            v = v_ref[pl.ds(ks, kw), :]
            s = lax.dot_general(q, k, (((1,), (1,)), ((), ())), preferred_element_type=F32) * scale
            qrow = g0 + lax.broadcasted_iota(jnp.int32, (bq, kw), 0)
            krow = ks + lax.broadcasted_iota(jnp.int32, (bq, kw), 1)
            lo, n = _segment(g0, p_len, s_len)
            valid = (jnp.abs(krow - qrow) <= BAND_W) & (krow >= lo) & (krow < lo + n)
            s = jnp.where(valid, s, NEG_BIG)
            mx = jnp.max(s, axis=-1, keepdims=True)
            p = jnp.exp(s - mx)
            l = jnp.sum(p, axis=-1, keepdims=True)
            pv = jnp.dot(p.astype(BF16), v, preferred_element_type=F32)
            o_ref[pl.ds(s0, bq), :] = pv / l
            lse_ref[pl.ds(s0, bq), :] = jnp.broadcast_to(mx + jnp.log(l), (bq, LANES))
            return carry

        lax.fori_loop(0, nsub, body, 0)

    def col(off):
        return lambda r, j, c: (0, r * blocks_per_row + off + g * hg + j)

    q_spec = pl.BlockSpec((rc, LANES), lambda r, j, c: (c, r * blocks_per_row + g * hg + j))
    k_spec = pl.BlockSpec((rows, LANES), col(aw // LANES))
    v_spec = pl.BlockSpec((rows, LANES), col(2 * aw // LANES))
    o_spec = pl.BlockSpec((rc, LANES), lambda r, j, c: (c, r * hg + j))
    shp = jax.ShapeDtypeStruct((rows, dil * hg * LANES), F32)
    out, lse = pl.pallas_call(
        kernel, grid=(dil, hg, rows // rc), in_specs=[q_spec, k_spec, v_spec],
        out_specs=[o_spec, o_spec], out_shape=[shp, shp],
        compiler_params=_params(("parallel", "parallel", "arbitrary")), name=f"banded_attention_g{g}",
    )(view, view, view)
    return out.reshape(m, hg * LANES), lse.reshape(m, hg * LANES)


def _mix_groups(outs, lses, cfg):
    m = cfg.m
    gw = cfg.hg * LANES
    tm = _tile(m, 256)
    ng = len(outs)

    def kernel(*refs):
        o_refs, l_refs, out_ref = refs[:ng], refs[ng:2 * ng], refs[2 * ng]
        ls = [r[...] for r in l_refs]
        mx = functools.reduce(jnp.maximum, ls)
        es = [jnp.exp(l - mx) for l in ls]
        inv = 1.0 / functools.reduce(jnp.add, es)
        for gi in range(ng):
            out_ref[:, gi * gw:(gi + 1) * gw] = (o_refs[gi][...] * (es[gi] * inv)).astype(BF16)

    spec = pl.BlockSpec((tm, gw), lambda i: (i, 0))
    return pl.pallas_call(
        kernel, grid=(m // tm,), in_specs=[spec] * (2 * ng),
        out_specs=pl.BlockSpec((tm, ng * gw), lambda i: (i, 0)),
        out_shape=jax.ShapeDtypeStruct((m, ng * gw), BF16),
        compiler_params=_params(("parallel",)), name="mix_groups",
    )(*outs, *lses)


def _latent_attention(q, kv, kpe, cfg):
    m, h = cfg.m, cfg.b_heads
    p_len, s_len = cfg.prompt_len, cfg.sample_len
    tq = _tile(s_len, 512)
    tk = _tile(s_len, 512)

    def kernel(q_ref, kn_ref, kpe_ref, v_ref, o_ref):
        row0 = pl.program_id(1) * tq
        lo, n = _segment(row0, p_len, s_len)
        qt = q_ref[...]

        def body(c, carry):
            mx, l, acc = carry
            ks = pl.multiple_of(lo + c * tk, tk)
            k = jnp.concatenate([kn_ref[pl.ds(ks, tk), :], kpe_ref[pl.ds(ks, tk), :]], axis=1)
            s = lax.dot_general(qt, k, (((1,), (1,)), ((), ())), preferred_element_type=F32)
            mx_new = jnp.maximum(mx, jnp.max(s, axis=-1, keepdims=True))
            corr = jnp.exp(mx - mx_new)
            p = jnp.exp(s - mx_new)
            l = corr * l + jnp.sum(p, axis=-1, keepdims=True)
            acc = corr * acc + jnp.dot(p.astype(BF16), v_ref[pl.ds(ks, tk), :],
                                       preferred_element_type=F32)
            return mx_new, l, acc

        init = (jnp.full((tq, 1), -jnp.inf, F32), jnp.zeros((tq, 1), F32), jnp.zeros((tq, V_DIM), F32))
        _, l, acc = lax.fori_loop(0, n // tk, body, init)
        o_ref[...] = (acc / l).astype(BF16)

    q_spec = pl.BlockSpec((tq, Q_HEAD_PAD), lambda hh, i: (i, hh))
    kn_spec = pl.BlockSpec((m, LANES), lambda hh, i: (0, 2 * hh))
    kpe_spec = pl.BlockSpec((m, LANES), lambda hh, i: (0, 0))
    v_spec = pl.BlockSpec((m, LANES), lambda hh, i: (0, 2 * hh + 1))
    o_spec = pl.BlockSpec((tq, V_DIM), lambda hh, i: (i, hh))
    return pl.pallas_call(
        kernel, grid=(h, m // tq), in_specs=[q_spec, kn_spec, kpe_spec, v_spec], out_specs=o_spec,
        out_shape=jax.ShapeDtypeStruct((m, h * V_DIM), BF16),
        compiler_params=_params(("parallel", "arbitrary")), name="latent_attention",
    )(q, kv, kpe, kv)


def _rope_tables(pos, rot_dim):
    half = rot_dim // 2
    inv_freq = ROPE_THETA ** (-jnp.arange(half, dtype=F32) / half)
    ang = pos[:, None] * inv_freq[None, :]
    cos, sin = jnp.cos(ang), jnp.sin(ang)
    rest = LANES - rot_dim
    ones = jnp.ones((pos.shape[0], rest), F32)
    zeros = jnp.zeros((pos.shape[0], rest), F32)
    zh = jnp.zeros_like(sin)
    c = jnp.concatenate([cos, cos, ones], axis=1)
    sl = jnp.concatenate([-sin, zh, zeros], axis=1)
    sr = jnp.concatenate([zh, sin, zeros], axis=1)
    return c, sl, sr


def _prep_weights(cfg, w_in, w_uq, w_ukv, w_ba, w_bb, w_o, w_gate, w_up, w_down):
    d, aw, ql, kl = cfg.d_model, cfg.a_width, cfg.q_lora, cfg.kv_lora
    off_cq = 3 * aw
    off_ga = off_cq + ql + kl + ROPE_DIM
    w_qkv = w_in[:, :off_cq].astype(BF16)
    w_lat = jnp.pad(w_in[:, off_cq:off_ga], ((0, 0), (0, LANES - ROPE_DIM))).astype(BF16)
    w_g = w_in[:, off_ga:].astype(BF16)
    w_uq_pad = jnp.pad(w_uq.reshape(ql, cfg.b_heads, QK_DIM),
                       ((0, 0), (0, 0), (0, Q_HEAD_PAD - QK_DIM))).reshape(ql, -1).astype(BF16)
    fpad = cfg.ff_pad - cfg.ff
    return dict(
        w_qkv=w_qkv, w_lat=w_lat, w_g=w_g, w_uq=w_uq_pad, w_ukv=w_ukv.astype(BF16),
        w_ba=w_ba.astype(BF16), w_bb=w_bb.astype(BF16), w_o=w_o.astype(BF16),
        w_gate=jnp.pad(w_gate, ((0, 0), (0, fpad))).astype(BF16),
        w_up=jnp.pad(w_up, ((0, 0), (0, fpad))).astype(BF16),
        w_down=jnp.pad(w_down, ((0, fpad), (0, 0))).astype(BF16),
    )


def _layer(x, x_bf, w, q_norm, kv_norm, ln1_g, ln1_b, ln2_g, ln2_b, tabs_a, tabs_b, cfg):
    qkv = _proj_qkv(x_bf, w["w_qkv"], tabs_a, cfg)
    c_q, c_kv, kpe = _proj_latent(x_bf, w["w_lat"], q_norm, kv_norm, tabs_b, cfg)
    gates = _proj_gates(x_bf, w["w_g"], cfg)

    outs, lses = [], []
    for g, (_, dil) in enumerate(DIL_GROUPS):
        o, s = _banded_attention(qkv, g, dil, cfg)
        outs.append(o)
        lses.append(s)
    out_a = _mix_groups(outs, lses, cfg)

    q = _proj_q(c_q, w["w_uq"], tabs_b, cfg)
    kv = _proj_kv(c_kv, w["w_ukv"], cfg)
    out_b = _latent_attention(q, kv, kpe, cfg)

    part_a = _branch_a(out_a, w["w_ba"], gates, cfg)
    merged = _branch_b_merge(out_b, w["w_bb"], gates, part_a, cfg)
    y1 = _residual_matmul(merged, w["w_o"], x, cfg, tk=None, name="out_proj")
    x1, x1_bf = _layer_norm(y1, ln1_g, ln1_b, cfg)

    hmid = _swiglu(x1_bf, w["w_gate"], w["w_up"], cfg)
    y2 = _residual_matmul(hmid, w["w_down"], x1, cfg, tk=_tile(cfg.ff_pad, 1024), name="ffn_down")
    return _layer_norm(y2, ln2_g, ln2_b, cfg)


def _trunk(cfg, x_prompt, x_sample, w_in, mla_q_norm, w_uq, mla_kv_norm, w_ukv, w_branch_a, w_branch_b,
           w_out, ln1_g, ln1_b, w_ffn_gate, w_ffn_up, w_ffn_down, ln2_g, ln2_b):
    d = cfg.d_model
    x = jnp.concatenate([x_prompt.reshape(-1, d), x_sample.reshape(-1, d)], axis=0)
    x_bf = x.astype(BF16)
    pos = jnp.concatenate([jnp.arange(cfg.prompt_len, dtype=F32),
                           jnp.tile(jnp.arange(cfg.sample_len, dtype=F32), cfg.n_sample)])
    tabs_a = _rope_tables(pos, A_ROT_DIM)
    tabs_b = _rope_tables(pos, ROPE_DIM)
    for l in range(cfg.depth):
        w = _prep_weights(cfg, w_in[l], w_uq[l], w_ukv[l], w_branch_a[l], w_branch_b[l], w_out[l],
                          w_ffn_gate[l], w_ffn_up[l], w_ffn_down[l])
        x, x_bf = _layer(x, x_bf, w, mla_q_norm[l], mla_kv_norm[l], ln1_g[l], ln1_b[l], ln2_g[l],
                         ln2_b[l], tabs_a, tabs_b, cfg)
    return (x[:cfg.prompt_len].reshape(x_prompt.shape), x[cfg.prompt_len:].reshape(x_sample.shape))


def kernel(x_prompt, x_sample, w_in, mla_q_norm, w_uq, mla_kv_norm, w_ukv, w_branch_a, w_branch_b, w_out,
           ln1_g, ln1_b, w_ffn_gate, w_ffn_up, w_ffn_down, ln2_g, ln2_b):
    assert x_prompt.shape[0] == 1, "one prompt sequence"
    cfg = Cfg(d_model=x_prompt.shape[-1], prompt_len=x_prompt.shape[1], sample_len=x_sample.shape[1],
              n_sample=x_sample.shape[0], depth=w_in.shape[0],
              hg=w_branch_a.shape[1] // (len(DIL_GROUPS) * A_HEAD_DIM),
              b_heads=w_branch_b.shape[1] // V_DIM, q_lora=w_uq.shape[1], kv_lora=w_ukv.shape[1],
              ff=w_ffn_gate.shape[2])
    return _trunk(cfg, x_prompt, x_sample, w_in, mla_q_norm, w_uq, mla_kv_norm, w_ukv, w_branch_a,
                  w_branch_b, w_out, ln1_g, ln1_b, w_ffn_gate, w_ffn_up, w_ffn_down, ln2_g, ln2_b)
```

```python
import functools
from typing import NamedTuple

import jax
import jax.numpy as jnp
from jax import lax
from jax.experimental import pallas as pl
from jax.experimental.pallas import tpu as pltpu

F32 = jnp.float32
BF16 = jnp.bfloat16

LANES = 128
ROPE_THETA = 500000.0
LN_EPS = 1e-5
RMS_EPS = 1e-6
NEG_BIG = -1e30
LOG2_E = 1.4426950408889634

DIL_GROUPS = ((128, 1), (512, 4), (2048, 16))
BAND_W = 64
A_HEAD_DIM = 128
A_ROT_DIM = 32
NOPE_DIM = 128
ROPE_DIM = 64
V_DIM = 128
QK_DIM = NOPE_DIM + ROPE_DIM
Q_HEAD_PAD = 256
VMEM_LIMIT_MB = 56


class Cfg(NamedTuple):
    d_model: int
    prompt_len: int
    sample_len: int
    n_sample: int
    depth: int
    hg: int
    b_heads: int
    q_lora: int
    kv_lora: int
    ff: int

    @property
    def m(self):
        return self.prompt_len + self.n_sample * self.sample_len

    @property
    def a_width(self):
        return len(DIL_GROUPS) * self.hg * A_HEAD_DIM

    @property
    def ff_pad(self):
        return -(-self.ff // 1024) * 1024 if self.ff > 1024 else -(-self.ff // LANES) * LANES

    @property
    def alpha(self):
        return (2.0 * self.depth) ** 0.25


def _tile(n, pref):
    if n <= pref:
        return n
    t = (pref // LANES) * LANES
    while n % t:
        t -= LANES
    return t


def _params(sem):
    return pltpu.CompilerParams(dimension_semantics=sem,
                                vmem_limit_bytes=VMEM_LIMIT_MB * 1024 * 1024)


def _matmul(a, bs, extras, out_defs, epilogue, *, tm, tn, name):
    m, k_dim = a.shape
    n = bs[0].shape[1]
    nb, ne, no = len(bs), len(extras), len(out_defs)
    swap = lambda f: (lambda j, i: f(i, j))

    a_spec = pl.BlockSpec((tm, k_dim), lambda j, i: (i, 0))
    b_spec = pl.BlockSpec((k_dim, tn), lambda j, i: (0, j))
    in_specs = [a_spec] + [b_spec] * nb + [pl.BlockSpec(bshape, swap(f)) for _, bshape, f in extras]
    out_specs = [pl.BlockSpec((tm, bc), lambda j, i: (i, j)) for _, _, bc in out_defs]
    out_shape = [jax.ShapeDtypeStruct((m, nc), dt) for nc, dt, _ in out_defs]

    def kernel(*refs):
        a_ref = refs[0]
        b_refs = refs[1:1 + nb]
        e_refs = refs[1 + nb:1 + nb + ne]
        o_refs = refs[1 + nb + ne:]
        accs = [jnp.dot(a_ref[...], b[...], preferred_element_type=F32) for b in b_refs]
        epilogue(accs, e_refs, o_refs, pl.program_id(0))

    return pl.pallas_call(
        kernel, grid=(n // tn, m // tm), in_specs=in_specs, out_specs=out_specs, out_shape=out_shape,
        compiler_params=_params(("parallel", "arbitrary")), name=name,
    )(a, *bs, *[e[0] for e in extras])


def _rope_lanes(x, c, sl, sr, half):
    return x * c + pltpu.roll(x, LANES - half, 1) * sl + pltpu.roll(x, half, 1) * sr


def _row_tables(tabs, tm):
    return [(t, (tm, LANES), lambda i, j: (i, 0)) for t in tabs]


def _proj_qkv(x_bf, w_qkv, tabs_a, cfg):
    tm, tn = _tile(cfg.m, 1024), _tile(cfg.a_width, 1024)
    n_rope_blocks = 2 * cfg.a_width // tn
    half = A_ROT_DIM // 2

    def epilogue(accs, e, o, j):
        acc = accs[0]

        @pl.when(j < n_rope_blocks)
        def _():
            c, sl, sr = e[0][...], e[1][...], e[2][...]
            for h in range(tn // LANES):
                hs = slice(h * LANES, (h + 1) * LANES)
                o[0][:, hs] = _rope_lanes(acc[:, hs], c, sl, sr, half).astype(BF16)

        @pl.when(j >= n_rope_blocks)
        def _():
            o[0][...] = acc.astype(BF16)

    return _matmul(x_bf, [w_qkv], _row_tables(tabs_a, tm), [(3 * cfg.a_width, BF16, tn)], epilogue,
                   tm=tm, tn=tn, name="proj_qkv")[0]


def _proj_latent(x_bf, w_lat, q_norm, kv_norm, tabs_b, cfg):
    ql, kl = cfg.q_lora, cfg.kv_lora
    n = ql + kl + LANES
    tm = _tile(cfg.m, 512)
    half = ROPE_DIM // 2

    def rms(v, g):
        return v * lax.rsqrt(jnp.mean(v * v, axis=-1, keepdims=True) + RMS_EPS) * g

    def epilogue(accs, e, o, j):
        acc = accs[0]
        o[0][...] = rms(acc[:, :ql], e[0][...]).astype(BF16)
        o[1][...] = rms(acc[:, ql:ql + kl], e[1][...]).astype(BF16)
        o[2][...] = _rope_lanes(acc[:, ql + kl:], e[2][...], e[3][...], e[4][...], half).astype(BF16)

    extras = [(q_norm.reshape(1, ql), (1, ql), lambda i, j: (0, 0)),
              (kv_norm.reshape(1, kl), (1, kl), lambda i, j: (0, 0))] + _row_tables(tabs_b, tm)
    return _matmul(x_bf, [w_lat], extras, [(ql, BF16, ql), (kl, BF16, kl), (LANES, BF16, LANES)],
                   epilogue, tm=tm, tn=n, name="proj_latent")


def _proj_gates(x_bf, w_g, cfg):
    tm, tn = _tile(cfg.m, 1024), _tile(2 * cfg.d_model, 1024)

    def epilogue(accs, e, o, j):
        o[0][...] = jax.nn.sigmoid(accs[0])

    return _matmul(x_bf, [w_g], [], [(2 * cfg.d_model, F32, tn)], epilogue, tm=tm, tn=tn,
                   name="proj_gates")[0]


def _proj_q(c_q, w_uq_pad, tabs_b, cfg):
    tm, tn = _tile(cfg.m, 1024), _tile(cfg.b_heads * Q_HEAD_PAD, 1024)
    half = ROPE_DIM // 2
    scale = QK_DIM ** -0.5 * LOG2_E

    def epilogue(accs, e, o, j):
        acc = accs[0]
        c, sl, sr = e[0][...], e[1][...], e[2][...]
        for h in range(tn // Q_HEAD_PAD):
            lo = h * Q_HEAD_PAD
            o[0][:, lo:lo + LANES] = (acc[:, lo:lo + LANES] * scale).astype(BF16)
            pe = _rope_lanes(acc[:, lo + LANES:lo + 2 * LANES], c, sl, sr, half)
            o[0][:, lo + LANES:lo + 2 * LANES] = (pe * scale).astype(BF16)

    return _matmul(c_q, [w_uq_pad], _row_tables(tabs_b, tm), [(cfg.b_heads * Q_HEAD_PAD, BF16, tn)],
                   epilogue, tm=tm, tn=tn, name="proj_q")[0]


def _proj_k(c_kv, w_uk, cfg):
    tm, tn = _tile(cfg.m, 1024), _tile(cfg.b_heads * NOPE_DIM, 1024)

    def epilogue(accs, e, o, j):
        o[0][...] = accs[0].astype(BF16)

    return _matmul(c_kv, [w_uk], [], [(cfg.b_heads * NOPE_DIM, BF16, tn)], epilogue, tm=tm, tn=tn,
                   name="proj_k")[0]


def _proj_vt(c_kv, w_uv_t, cfg):
    m, kl = c_kv.shape
    n = w_uv_t.shape[0]
    tm, tn = _tile(m, 1024), _tile(n, 1024)

    def kernel(w_ref, c_ref, o_ref):
        o_ref[...] = lax.dot_general(w_ref[...], c_ref[...], (((1,), (1,)), ((), ())),
                                     preferred_element_type=F32).astype(BF16)

    return pl.pallas_call(
        kernel, grid=(m // tm, n // tn),
        in_specs=[pl.BlockSpec((tn, kl), lambda i, j: (j, 0)), pl.BlockSpec((tm, kl), lambda i, j: (i, 0))],
        out_specs=pl.BlockSpec((tn, tm), lambda i, j: (j, i)),
        out_shape=jax.ShapeDtypeStruct((n, m), BF16),
        compiler_params=_params(("parallel", "arbitrary")), name="proj_vt",
    )(w_uv_t, c_kv)


def _branch_a(out_a, w_ba, gates, cfg):
    tm, tn = _tile(cfg.m, 1024), _tile(cfg.d_model, 512)

    def epilogue(accs, e, o, j):
        o[0][...] = e[0][...] * accs[0]

    extras = [(gates, (tm, tn), lambda i, j: (i, j))]
    return _matmul(out_a, [w_ba], extras, [(cfg.d_model, F32, tn)], epilogue, tm=tm, tn=tn,
                   name="branch_a")[0]


def _branch_b_merge(out_b, w_bb, gates, part_a, cfg):
    tm, tn = _tile(cfg.m, 512), _tile(cfg.d_model, 512)
    goff = cfg.d_model // tn

    def epilogue(accs, e, o, j):
        o[0][...] = (e[1][...] + e[0][...] * accs[0]).astype(BF16)

    extras = [(gates, (tm, tn), lambda i, j: (i, j + goff)),
              (part_a, (tm, tn), lambda i, j: (i, j))]
    return _matmul(out_b, [w_bb], extras, [(cfg.d_model, BF16, tn)], epilogue, tm=tm, tn=tn,
                   name="branch_b_merge")[0]


def _residual_matmul(a, w, x_res, cfg, *, tm, name):
    tm, tn = _tile(cfg.m, tm), _tile(cfg.d_model, 512)
    alpha = cfg.alpha

    def epilogue(accs, e, o, j):
        o[0][...] = alpha * e[0][...] + accs[0]

    extras = [(x_res, (tm, tn), lambda i, j: (i, j))]
    return _matmul(a, [w], extras, [(cfg.d_model, F32, tn)], epilogue, tm=tm, tn=tn, name=name)[0]


def _swiglu(x_bf, w_gate, w_up, cfg):
    tm, tn = _tile(cfg.m, 1024), _tile(cfg.ff_pad, 512)

    def epilogue(accs, e, o, j):
        g, u = accs
        o[0][...] = (g * jax.nn.sigmoid(g) * u).astype(BF16)

    return _matmul(x_bf, [w_gate, w_up], [], [(cfg.ff_pad, BF16, tn)], epilogue, tm=tm, tn=tn,
                   name="swiglu")[0]


def _layer_norm(y, g, b, cfg):
    m, d = y.shape
    tm = _tile(m, 256)

    def kernel(y_ref, g_ref, b_ref, o_ref, obf_ref):
        v = y_ref[...]
        mu = jnp.mean(v, axis=-1, keepdims=True)
        vc = v - mu
        var = jnp.mean(vc * vc, axis=-1, keepdims=True)
        out = vc * lax.rsqrt(var + LN_EPS) * g_ref[...] + b_ref[...]
        o_ref[...] = out
        obf_ref[...] = out.astype(BF16)

    row = pl.BlockSpec((tm, d), lambda i: (i, 0))
    vec = pl.BlockSpec((1, d), lambda i: (0, 0))
    return pl.pallas_call(
        kernel, grid=(m // tm,), in_specs=[row, vec, vec], out_specs=[row, row],
        out_shape=[jax.ShapeDtypeStruct((m, d), F32), jax.ShapeDtypeStruct((m, d), BF16)],
        compiler_params=_params(("parallel",)), name="layer_norm",
    )(y, g.reshape(1, d), b.reshape(1, d))


def _segment(row0, p_len, s_len):
    in_prompt = row0 < p_len
    s_idx = jnp.maximum(row0 - p_len, 0) // s_len
    lo = jnp.where(in_prompt, 0, p_len + s_idx * s_len)
    return lo, jnp.where(in_prompt, p_len, s_len)


def _banded_attention(qkv, g, dil, cfg):
    m = cfg.m
    hg = cfg.hg
    aw = cfg.a_width
    rows = m // dil
    p_len, s_len = cfg.prompt_len // dil, cfg.sample_len // dil
    bq = 2 * BAND_W
    kw = bq + 2 * BAND_W
    rc = _tile(rows, 2048)
    nsub = rc // bq
    blocks_per_row = 3 * aw // LANES
    scale = A_HEAD_DIM ** -0.5
    view = qkv.reshape(rows, dil * 3 * aw)

    def kernel(q_ref, k_ref, v_ref, o_ref, lse_ref):
        base = pl.program_id(2) * rc

        def body(t, carry):
            s0 = pl.multiple_of(t * bq, bq)
            g0 = base + s0
            ks = pl.multiple_of(jnp.clip(g0 - BAND_W, 0, rows - kw), BAND_W)
            q = q_ref[pl.ds(s0, bq), :]
            k = k_ref[pl.ds(ks, kw), :]
            v = v_ref[pl.ds(ks, kw), :]
            s = lax.dot_general(q, k, (((1,), (1,)), ((), ())), preferred_element_type=F32) * scale
            qrow = g0 + lax.broadcasted_iota(jnp.int32, (bq, kw), 0)
            krow = ks + lax.broadcasted_iota(jnp.int32, (bq, kw), 1)
            lo, n = _segment(g0, p_len, s_len)
            valid = (jnp.abs(krow - qrow) <= BAND_W) & (krow >= lo) & (krow < lo + n)
            s = jnp.where(valid, s, NEG_BIG)
            mx = jnp.max(s, axis=-1, keepdims=True)
            p = jnp.exp(s - mx)
            l = jnp.sum(p, axis=-1, keepdims=True)
            pv = jnp.dot(p.astype(BF16), v, preferred_element_type=F32)
            o_ref[pl.ds(s0, bq), :] = pv / l
            lse_ref[pl.ds(s0, bq), :] = jnp.broadcast_to(mx + jnp.log(l), (bq, LANES))
            return carry

        lax.fori_loop(0, nsub, body, 0)

    def col(off):
        return lambda r, j, c: (0, r * blocks_per_row + off + g * hg + j)

    q_spec = pl.BlockSpec((rc, LANES), lambda r, j, c: (c, r * blocks_per_row + g * hg + j))
    k_spec = pl.BlockSpec((rows, LANES), col(aw // LANES))
    v_spec = pl.BlockSpec((rows, LANES), col(2 * aw // LANES))
    o_spec = pl.BlockSpec((rc, LANES), lambda r, j, c: (c, r * hg + j))
    shp = jax.ShapeDtypeStruct((rows, dil * hg * LANES), F32)
    out, lse = pl.pallas_call(
        kernel, grid=(dil, hg, rows // rc), in_specs=[q_spec, k_spec, v_spec],
        out_specs=[o_spec, o_spec], out_shape=[shp, shp],
        compiler_params=_params(("parallel", "parallel", "arbitrary")), name=f"banded_attention_g{g}",
    )(view, view, view)
    return out.reshape(m, hg * LANES), lse.reshape(m, hg * LANES)


def _mix_groups(outs, lses, cfg):
    m = cfg.m
    gw = cfg.hg * LANES
    tm = _tile(m, 256)
    ng = len(outs)

    def kernel(*refs):
        o_refs, l_refs, out_ref = refs[:ng], refs[ng:2 * ng], refs[2 * ng]
        ls = [r[...] for r in l_refs]
        mx = functools.reduce(jnp.maximum, ls)
        es = [jnp.exp(l - mx) for l in ls]
        inv = 1.0 / functools.reduce(jnp.add, es)
        for gi in range(ng):
            out_ref[:, gi * gw:(gi + 1) * gw] = (o_refs[gi][...] * (es[gi] * inv)).astype(BF16)

    spec = pl.BlockSpec((tm, gw), lambda i: (i, 0))
    return pl.pallas_call(
        kernel, grid=(m // tm,), in_specs=[spec] * (2 * ng),
        out_specs=pl.BlockSpec((tm, ng * gw), lambda i: (i, 0)),
        out_shape=jax.ShapeDtypeStruct((m, ng * gw), BF16),
        compiler_params=_params(("parallel",)), name="mix_groups",
    )(*outs, *lses)


def _latent_attention(q, kn, kpe, vt, cfg):
    m, h = cfg.m, cfg.b_heads
    p_len, s_len = cfg.prompt_len, cfg.sample_len
    tq = _tile(s_len, 512)
    tk = _tile(s_len // 4, 512)
    assert s_len % (4 * tk) == 0 and p_len % (4 * tk) == 0, "each sequence needs >= 2 chunk pairs"

    def kernel(q_ref, kn_ref, kpe_ref, vt_ref, o_ref, s_a, s_b, p_a, p_b, acc_ref):
        row0 = pl.program_id(1) * tq
        lo, n = _segment(row0, p_len, s_len)
        npair = n // (2 * tk)
        qt = q_ref[...]

        def scores(c, s_ref):
            ks = pl.multiple_of(lo + c * tk, tk)
            k = jnp.concatenate([kn_ref[pl.ds(ks, tk), :], kpe_ref[pl.ds(ks, tk), :]], axis=1)
            s_ref[...] = lax.dot_general(k, qt, (((1,), (1,)), ((), ())), preferred_element_type=F32)

        def softmax(s_ref, p_ref, mx, l):
            s = s_ref[...]
            mx_new = jnp.maximum(mx, jnp.max(s, axis=0, keepdims=True))
            corr = jnp.exp2(mx - mx_new)
            p = jnp.exp2(s - mx_new)
            p_ref[...] = p.astype(BF16)
            return mx_new, corr * l + jnp.sum(p, axis=0, keepdims=True), corr

        def values(c, p_ref, corr):
            ks = pl.multiple_of(lo + c * tk, tk)
            acc_ref[...] = corr * acc_ref[...] + jnp.dot(vt_ref[:, pl.ds(ks, tk)], p_ref[...],
                                                         preferred_element_type=F32)

        def pair(i, carry, first=False, last=False):
            mx, l, corr_prev = carry
            c = 2 * i
            scores(c + 1, s_b)
            mx, l, corr_a = softmax(s_a, p_a, mx, l)
            if not first:
                values(c - 1, p_b, corr_prev)
            if not last:
                scores(c + 2, s_a)
            mx, l, corr_b = softmax(s_b, p_b, mx, l)
            values(c, p_a, corr_a)
            return mx, l, corr_b

        acc_ref[...] = jnp.zeros_like(acc_ref)
        scores(0, s_a)
        row = jnp.zeros((1, tq), F32)
        carry = pair(0, (row - jnp.inf, row, row), first=True)
        carry = lax.fori_loop(1, npair - 1, pair, carry)
        _, l, corr = pair(npair - 1, carry, last=True)
        values(2 * npair - 1, p_b, corr)
        o_ref[...] = (acc_ref[...] / l).T.astype(BF16)

    scratch = [pltpu.VMEM((tk, tq), F32), pltpu.VMEM((tk, tq), F32), pltpu.VMEM((tk, tq), BF16),
               pltpu.VMEM((tk, tq), BF16), pltpu.VMEM((V_DIM, tq), F32)]
    q_spec = pl.BlockSpec((tq, Q_HEAD_PAD), lambda hh, i: (i, hh))
    kn_spec = pl.BlockSpec((m, LANES), lambda hh, i: (0, hh))
    kpe_spec = pl.BlockSpec((m, LANES), lambda hh, i: (0, 0))
    vt_spec = pl.BlockSpec((V_DIM, m), lambda hh, i: (hh, 0))
    o_spec = pl.BlockSpec((tq, V_DIM), lambda hh, i: (i, hh))
    return pl.pallas_call(
        kernel, grid=(h, m // tq), in_specs=[q_spec, kn_spec, kpe_spec, vt_spec], out_specs=o_spec,
        out_shape=jax.ShapeDtypeStruct((m, h * V_DIM), BF16), scratch_shapes=scratch,
        compiler_params=_params(("parallel", "arbitrary")), name="latent_attention",
    )(q, kn, kpe, vt)


def _rope_tables(pos, rot_dim):
    half = rot_dim // 2
    inv_freq = ROPE_THETA ** (-jnp.arange(half, dtype=F32) / half)
    ang = pos[:, None] * inv_freq[None, :]
    cos, sin = jnp.cos(ang), jnp.sin(ang)
    rest = LANES - rot_dim
    ones = jnp.ones((pos.shape[0], rest), F32)
    zeros = jnp.zeros((pos.shape[0], rest), F32)
    zh = jnp.zeros_like(sin)
    c = jnp.concatenate([cos, cos, ones], axis=1)
    sl = jnp.concatenate([-sin, zh, zeros], axis=1)
    sr = jnp.concatenate([zh, sin, zeros], axis=1)
    return c, sl, sr


def _prep_weights(cfg, w_in, w_uq, w_ukv, w_ba, w_bb, w_o, w_gate, w_up, w_down):
    d, aw, ql, kl = cfg.d_model, cfg.a_width, cfg.q_lora, cfg.kv_lora
    off_cq = 3 * aw
    off_ga = off_cq + ql + kl + ROPE_DIM
    w_qkv = w_in[:, :off_cq].astype(BF16)
    w_lat = jnp.pad(w_in[:, off_cq:off_ga], ((0, 0), (0, LANES - ROPE_DIM))).astype(BF16)
    w_g = w_in[:, off_ga:].astype(BF16)
    w_uq_pad = jnp.pad(w_uq.reshape(ql, cfg.b_heads, QK_DIM),
                       ((0, 0), (0, 0), (0, Q_HEAD_PAD - QK_DIM))).reshape(ql, -1).astype(BF16)
    fpad = cfg.ff_pad - cfg.ff
    w_ukv3 = w_ukv.reshape(kl, cfg.b_heads, NOPE_DIM + V_DIM)
    w_uk = w_ukv3[:, :, :NOPE_DIM].reshape(kl, -1).astype(BF16)
    w_uv_t = w_ukv3[:, :, NOPE_DIM:].reshape(kl, -1).T.astype(BF16)
    return dict(
        w_qkv=w_qkv, w_lat=w_lat, w_g=w_g, w_uq=w_uq_pad, w_uk=w_uk, w_uv_t=w_uv_t,
        w_ba=w_ba.astype(BF16), w_bb=w_bb.astype(BF16), w_o=w_o.astype(BF16),
        w_gate=jnp.pad(w_gate, ((0, 0), (0, fpad))).astype(BF16),
        w_up=jnp.pad(w_up, ((0, 0), (0, fpad))).astype(BF16),
        w_down=jnp.pad(w_down, ((0, fpad), (0, 0))).astype(BF16),
    )


def _layer(x, x_bf, w, q_norm, kv_norm, ln1_g, ln1_b, ln2_g, ln2_b, tabs_a, tabs_b, cfg):
    qkv = _proj_qkv(x_bf, w["w_qkv"], tabs_a, cfg)
    c_q, c_kv, kpe = _proj_latent(x_bf, w["w_lat"], q_norm, kv_norm, tabs_b, cfg)
    gates = _proj_gates(x_bf, w["w_g"], cfg)

    outs, lses = [], []
    for g, (_, dil) in enumerate(DIL_GROUPS):
        o, s = _banded_attention(qkv, g, dil, cfg)
        outs.append(o)
        lses.append(s)
    out_a = _mix_groups(outs, lses, cfg)

    q = _proj_q(c_q, w["w_uq"], tabs_b, cfg)
    kn = _proj_k(c_kv, w["w_uk"], cfg)
    vt = _proj_vt(c_kv, w["w_uv_t"], cfg)
    out_b = _latent_attention(q, kn, kpe, vt, cfg)

    part_a = _branch_a(out_a, w["w_ba"], gates, cfg)
    merged = _branch_b_merge(out_b, w["w_bb"], gates, part_a, cfg)
    y1 = _residual_matmul(merged, w["w_o"], x, cfg, tm=1024, name="out_proj")
    x1, x1_bf = _layer_norm(y1, ln1_g, ln1_b, cfg)

    hmid = _swiglu(x1_bf, w["w_gate"], w["w_up"], cfg)
    y2 = _residual_matmul(hmid, w["w_down"], x1, cfg, tm=256, name="ffn_down")
    return _layer_norm(y2, ln2_g, ln2_b, cfg)


def _trunk(cfg, x_prompt, x_sample, w_in, mla_q_norm, w_uq, mla_kv_norm, w_ukv, w_branch_a, w_branch_b,
           w_out, ln1_g, ln1_b, w_ffn_gate, w_ffn_up, w_ffn_down, ln2_g, ln2_b):
    d = cfg.d_model
    x = jnp.concatenate([x_prompt.reshape(-1, d), x_sample.reshape(-1, d)], axis=0)
    x_bf = x.astype(BF16)
    pos = jnp.concatenate([jnp.arange(cfg.prompt_len, dtype=F32),
                           jnp.tile(jnp.arange(cfg.sample_len, dtype=F32), cfg.n_sample)])
    tabs_a = _rope_tables(pos, A_ROT_DIM)
    tabs_b = _rope_tables(pos, ROPE_DIM)
    for l in range(cfg.depth):
        w = _prep_weights(cfg, w_in[l], w_uq[l], w_ukv[l], w_branch_a[l], w_branch_b[l], w_out[l],
                          w_ffn_gate[l], w_ffn_up[l], w_ffn_down[l])
        x, x_bf = _layer(x, x_bf, w, mla_q_norm[l], mla_kv_norm[l], ln1_g[l], ln1_b[l], ln2_g[l],
                         ln2_b[l], tabs_a, tabs_b, cfg)
    return (x[:cfg.prompt_len].reshape(x_prompt.shape), x[cfg.prompt_len:].reshape(x_sample.shape))


def kernel(x_prompt, x_sample, w_in, mla_q_norm, w_uq, mla_kv_norm, w_ukv, w_branch_a, w_branch_b, w_out,
           ln1_g, ln1_b, w_ffn_gate, w_ffn_up, w_ffn_down, ln2_g, ln2_b):
    assert x_prompt.shape[0] == 1, "one prompt sequence"
    cfg = Cfg(d_model=x_prompt.shape[-1], prompt_len=x_prompt.shape[1], sample_len=x_sample.shape[1],
              n_sample=x_sample.shape[0], depth=w_in.shape[0],
              hg=w_branch_a.shape[1] // (len(DIL_GROUPS) * A_HEAD_DIM),
              b_heads=w_branch_b.shape[1] // V_DIM, q_lora=w_uq.shape[1], kv_lora=w_ukv.shape[1],
              ff=w_ffn_gate.shape[2])
    return _trunk(cfg, x_prompt, x_sample, w_in, mla_q_norm, w_uq, mla_kv_norm, w_ukv, w_branch_a,
                  w_branch_b, w_out, ln1_g, ln1_b, w_ffn_gate, w_ffn_up, w_ffn_down, ln2_g, ln2_b)
```

```python
import functools
from typing import NamedTuple

import jax
import jax.numpy as jnp
from jax import lax
from jax.experimental import pallas as pl
from jax.experimental.pallas import tpu as pltpu

F32 = jnp.float32
BF16 = jnp.bfloat16

LANES = 128
BF16_ROWS = 16
ROPE_THETA = 500000.0
LN_EPS = 1e-5
RMS_EPS = 1e-6
NEG_BIG = -1e30
LOG2_E = 1.4426950408889634

DIL_GROUPS = ((128, 1), (512, 4), (2048, 16))
BAND_W = 64
A_HEAD_DIM = 128
A_ROT_DIM = 32
NOPE_DIM = 128
ROPE_DIM = 64
V_DIM = 128
QK_DIM = NOPE_DIM + ROPE_DIM
SUM_ROWS = BF16_ROWS
VMEM_LIMIT_MB = 56


class Cfg(NamedTuple):
    d_model: int
    prompt_len: int
    sample_len: int
    n_sample: int
    depth: int
    hg: int
    b_heads: int
    q_lora: int
    kv_lora: int
    ff: int

    @property
    def m(self):
        return self.prompt_len + self.n_sample * self.sample_len

    @property
    def gw(self):
        return self.hg * A_HEAD_DIM

    @property
    def a_width(self):
        return len(DIL_GROUPS) * self.gw

    @property
    def ff_pad(self):
        return -(-self.ff // 1024) * 1024 if self.ff > 1024 else -(-self.ff // LANES) * LANES

    @property
    def alpha(self):
        return (2.0 * self.depth) ** 0.25


def _tile(n, pref):
    if n <= pref:
        return n
    t = (pref // LANES) * LANES
    while n % t:
        t -= LANES
    return t


def _params(sem):
    return pltpu.CompilerParams(dimension_semantics=sem,
                                vmem_limit_bytes=VMEM_LIMIT_MB * 1024 * 1024)


def _matmul(a, bs, extras, out_defs, epilogue, *, tm, tn, name, n_col_blocks=None, b_block=None,
            scratch=()):
    m, k_dim = a.shape
    nj = bs[0].shape[1] // tn if n_col_blocks is None else n_col_blocks
    b_block = (lambda j: j) if b_block is None else b_block
    nb, ne, no = len(bs), len(extras), len(out_defs)
    swap = lambda f: (lambda j, i: f(i, j))

    a_spec = pl.BlockSpec((tm, k_dim), lambda j, i: (i, 0))
    b_spec = pl.BlockSpec((k_dim, tn), lambda j, i: (0, b_block(j)))
    in_specs = [a_spec] + [b_spec] * nb + [pl.BlockSpec(bshape, swap(f)) for _, bshape, f in extras]
    out_specs = [pl.BlockSpec(bshape, lambda j, i: (i, j)) for _, _, bshape in out_defs]
    out_shape = [jax.ShapeDtypeStruct(shape, dt) for shape, dt, _ in out_defs]

    def kernel(*refs):
        a_ref = refs[0]
        b_refs = refs[1:1 + nb]
        e_refs = refs[1 + nb:1 + nb + ne]
        o_refs = refs[1 + nb + ne:1 + nb + ne + no]
        s_refs = refs[1 + nb + ne + no:]
        accs = [jnp.dot(a_ref[...], b[...], preferred_element_type=F32) for b in b_refs]
        epilogue(accs, e_refs, o_refs, pl.program_id(0), s_refs)

    return pl.pallas_call(
        kernel, grid=(nj, m // tm), in_specs=in_specs, out_specs=out_specs, out_shape=out_shape,
        scratch_shapes=list(scratch), compiler_params=_params(("parallel", "arbitrary")), name=name,
    )(a, *bs, *[e[0] for e in extras])


def _rows_out(m, n, dtype, tm, tn):
    return ((m, n), dtype, (tm, tn))


def _rope_lanes(x, c, sl, sr, half):
    return x * c + pltpu.roll(x, LANES - half, 1) * sl + pltpu.roll(x, half, 1) * sr


def _row_tables(tabs, tm):
    return [(t, (tm, LANES), lambda i, j: (i, 0)) for t in tabs]


def _proj_qkv_group(x_bf, w_qkv, tabs_a, g, dil, cfg):
    m, gw = cfg.m, cfg.gw
    tm = _tile(m, 1024)
    half = A_ROT_DIM // 2
    ngroups = len(DIL_GROUPS)

    def epilogue(accs, e, o, j, scr):
        acc = accs[0]
        heads = [slice(h * LANES, (h + 1) * LANES) for h in range(gw // LANES)]

        def emit(piece):
            if dil == 1:
                for hs in heads:
                    o[0][:, hs] = piece(hs).astype(BF16)
                return
            for h, hs in enumerate(heads):
                scr[0][h] = piece(hs)
            for r in range(dil):
                for h in range(len(heads)):
                    lo = r * gw + h * LANES
                    o[0][:, lo:lo + LANES] = scr[0][h, pl.ds(r, tm // dil, stride=dil), :].astype(BF16)

        @pl.when(j < 2)
        def _():
            c, sl, sr = e[0][...], e[1][...], e[2][...]
            emit(lambda hs: _rope_lanes(acc[:, hs], c, sl, sr, half))

        @pl.when(j >= 2)
        def _():
            emit(lambda hs: acc[:, hs])

    out_def = ((m // dil, 3 * dil * gw), BF16, (tm // dil, dil * gw))
    scratch = [pltpu.VMEM((gw // LANES, tm, LANES), F32)] if dil > 1 else []
    return _matmul(x_bf, [w_qkv], _row_tables(tabs_a, tm), [out_def], epilogue, tm=tm, tn=gw,
                   n_col_blocks=3, b_block=lambda j: j * ngroups + g, scratch=scratch,
                   name=f"proj_qkv_g{g}")[0]


def _proj_latent(x_bf, w_lat, q_norm, kv_norm, tabs_b, cfg):
    m, ql, kl = cfg.m, cfg.q_lora, cfg.kv_lora
    n = ql + kl + LANES
    tm = _tile(m, 512)
    half = ROPE_DIM // 2

    def rms(v, g):
        return v * lax.rsqrt(jnp.mean(v * v, axis=-1, keepdims=True) + RMS_EPS) * g

    def epilogue(accs, e, o, j, scr):
        acc = accs[0]
        o[0][...] = rms(acc[:, :ql], e[0][...]).astype(BF16)
        o[1][...] = rms(acc[:, ql:ql + kl], e[1][...]).astype(BF16)
        o[2][...] = _rope_lanes(acc[:, ql + kl:], e[2][...], e[3][...], e[4][...], half).astype(BF16)

    extras = [(q_norm.reshape(1, ql), (1, ql), lambda i, j: (0, 0)),
              (kv_norm.reshape(1, kl), (1, kl), lambda i, j: (0, 0))] + _row_tables(tabs_b, tm)
    outs = [_rows_out(m, ql, BF16, tm, ql), _rows_out(m, kl, BF16, tm, kl),
            _rows_out(m, LANES, BF16, tm, LANES)]
    return _matmul(x_bf, [w_lat], extras, outs, epilogue, tm=tm, tn=n, name="proj_latent")


def _proj_gates(x_bf, w_g, cfg):
    tm, tn = _tile(cfg.m, 1024), _tile(2 * cfg.d_model, 1024)

    def epilogue(accs, e, o, j, scr):
        o[0][...] = jax.nn.sigmoid(accs[0])

    return _matmul(x_bf, [w_g], [], [_rows_out(cfg.m, 2 * cfg.d_model, F32, tm, tn)], epilogue,
                   tm=tm, tn=tn, name="proj_gates")[0]


def _proj_qt(c_q, w_uq_t, cos_t, sin_t, cfg):
    m, ql = c_q.shape
    n = w_uq_t.shape[0]
    heads_per_tile = min(4, cfg.b_heads)
    tn = heads_per_tile * QK_DIM
    tm = _tile(m, 1024)
    half = ROPE_DIM // 2
    scale = QK_DIM ** -0.5 * LOG2_E

    def kernel(w_ref, c_ref, cos_ref, sin_ref, o_ref):
        acc = lax.dot_general(w_ref[...], c_ref[...], (((1,), (1,)), ((), ())),
                              preferred_element_type=F32) * scale
        cos, sin = cos_ref[...], sin_ref[...]
        for h in range(heads_per_tile):
            lo = h * QK_DIM
            o_ref[lo:lo + NOPE_DIM, :] = acc[lo:lo + NOPE_DIM, :].astype(BF16)
            x1 = acc[lo + NOPE_DIM:lo + NOPE_DIM + half, :]
            x2 = acc[lo + NOPE_DIM + half:lo + QK_DIM, :]
            o_ref[lo + NOPE_DIM:lo + NOPE_DIM + half, :] = (x1 * cos - x2 * sin).astype(BF16)
            o_ref[lo + NOPE_DIM + half:lo + QK_DIM, :] = (x1 * sin + x2 * cos).astype(BF16)

    tab = pl.BlockSpec((half, tm), lambda j, i: (0, i))
    return pl.pallas_call(
        kernel, grid=(n // tn, m // tm),
        in_specs=[pl.BlockSpec((tn, ql), lambda j, i: (j, 0)), pl.BlockSpec((tm, ql), lambda j, i: (i, 0)),
                  tab, tab],
        out_specs=pl.BlockSpec((tn, tm), lambda j, i: (j, i)),
        out_shape=jax.ShapeDtypeStruct((n, m), BF16),
        compiler_params=_params(("parallel", "arbitrary")), name="proj_qt",
    )(w_uq_t, c_q, cos_t, sin_t)


def _proj_kv(c_kv, w_ukv, cfg):
    n = cfg.b_heads * (NOPE_DIM + V_DIM)
    tm, tn = _tile(cfg.m, 1024), _tile(n, 1024)

    def epilogue(accs, e, o, j, scr):
        o[0][...] = accs[0].astype(BF16)

    return _matmul(c_kv, [w_ukv], [], [_rows_out(cfg.m, n, BF16, tm, tn)], epilogue, tm=tm, tn=tn,
                   name="proj_kv")[0]


def _branch_a(out_a, w_ba, gates, cfg):
    tm, tn = _tile(cfg.m, 1024), _tile(cfg.d_model, 512)

    def epilogue(accs, e, o, j, scr):
        o[0][...] = e[0][...] * accs[0]

    extras = [(gates, (tm, tn), lambda i, j: (i, j))]
    return _matmul(out_a, [w_ba], extras, [_rows_out(cfg.m, cfg.d_model, F32, tm, tn)], epilogue,
                   tm=tm, tn=tn, name="branch_a")[0]


def _branch_b_merge(out_b, w_bb, gates, part_a, cfg):
    tm, tn = _tile(cfg.m, 512), _tile(cfg.d_model, 512)
    goff = cfg.d_model // tn

    def epilogue(accs, e, o, j, scr):
        o[0][...] = (e[1][...] + e[0][...] * accs[0]).astype(BF16)

    extras = [(gates, (tm, tn), lambda i, j: (i, j + goff)),
              (part_a, (tm, tn), lambda i, j: (i, j))]
    return _matmul(out_b, [w_bb], extras, [_rows_out(cfg.m, cfg.d_model, BF16, tm, tn)], epilogue,
                   tm=tm, tn=tn, name="branch_b_merge")[0]


def _residual_matmul(a, w, x_res, cfg, *, tm, name):
    tm, tn = _tile(cfg.m, tm), _tile(cfg.d_model, 512)
    alpha = cfg.alpha

    def epilogue(accs, e, o, j, scr):
        o[0][...] = alpha * e[0][...] + accs[0]

    extras = [(x_res, (tm, tn), lambda i, j: (i, j))]
    return _matmul(a, [w], extras, [_rows_out(cfg.m, cfg.d_model, F32, tm, tn)], epilogue,
                   tm=tm, tn=tn, name=name)[0]


def _swiglu(x_bf, w_gate, w_up, cfg):
    tm, tn = _tile(cfg.m, 1024), _tile(cfg.ff_pad, 512)

    def epilogue(accs, e, o, j, scr):
        g, u = accs
        o[0][...] = (g * jax.nn.sigmoid(g) * u).astype(BF16)

    return _matmul(x_bf, [w_gate, w_up], [], [_rows_out(cfg.m, cfg.ff_pad, BF16, tm, tn)], epilogue,
                   tm=tm, tn=tn, name="swiglu")[0]


def _layer_norm(y, g, b, cfg):
    m, d = y.shape
    tm = _tile(m, 256)

    def kernel(y_ref, g_ref, b_ref, o_ref, obf_ref):
        v = y_ref[...]
        mu = jnp.mean(v, axis=-1, keepdims=True)
        vc = v - mu
        var = jnp.mean(vc * vc, axis=-1, keepdims=True)
        out = vc * lax.rsqrt(var + LN_EPS) * g_ref[...] + b_ref[...]
        o_ref[...] = out
        obf_ref[...] = out.astype(BF16)

    row = pl.BlockSpec((tm, d), lambda i: (i, 0))
    vec = pl.BlockSpec((1, d), lambda i: (0, 0))
    return pl.pallas_call(
        kernel, grid=(m // tm,), in_specs=[row, vec, vec], out_specs=[row, row],
        out_shape=[jax.ShapeDtypeStruct((m, d), F32), jax.ShapeDtypeStruct((m, d), BF16)],
        compiler_params=_params(("parallel",)), name="layer_norm",
    )(y, g.reshape(1, d), b.reshape(1, d))


def _segment(row0, p_len, s_len):
    in_prompt = row0 < p_len
    s_idx = jnp.maximum(row0 - p_len, 0) // s_len
    lo = jnp.where(in_prompt, 0, p_len + s_idx * s_len)
    return lo, jnp.where(in_prompt, p_len, s_len)


def _banded_attention(qkv_g, g, dil, cfg):
    hg, gw = cfg.hg, cfg.gw
    rows = cfg.m // dil
    p_len, s_len = cfg.prompt_len // dil, cfg.sample_len // dil
    bq = 2 * BAND_W
    kw = bq + 2 * BAND_W
    rc = _tile(rows, 2048)
    nsub = rc // bq
    scale = A_HEAD_DIM ** -0.5

    def kernel(q_ref, k_ref, v_ref, o_ref, lse_ref):
        base = pl.program_id(2) * rc

        def body(t, carry):
            s0 = pl.multiple_of(t * bq, bq)
            g0 = base + s0
            ks = pl.multiple_of(jnp.clip(g0 - BAND_W, 0, rows - kw), BAND_W)
            q = q_ref[pl.ds(s0, bq), :]
            k = k_ref[pl.ds(ks, kw), :]
            v = v_ref[pl.ds(ks, kw), :]
            s = lax.dot_general(q, k, (((1,), (1,)), ((), ())), preferred_element_type=F32) * scale
            qrow = g0 + lax.broadcasted_iota(jnp.int32, (bq, kw), 0)
            krow = ks + lax.broadcasted_iota(jnp.int32, (bq, kw), 1)
            lo, n = _segment(g0, p_len, s_len)
            valid = (jnp.abs(krow - qrow) <= BAND_W) & (krow >= lo) & (krow < lo + n)
            s = jnp.where(valid, s, NEG_BIG)
            mx = jnp.max(s, axis=-1, keepdims=True)
            p = jnp.exp(s - mx)
            l = jnp.sum(p, axis=-1, keepdims=True)
            pv = jnp.dot(p.astype(BF16), v, preferred_element_type=F32)
            o_ref[pl.ds(s0, bq), :] = pv / l
            lse_ref[pl.ds(s0, bq), :] = jnp.broadcast_to(mx + jnp.log(l), (bq, LANES))
            return carry

        lax.fori_loop(0, nsub, body, 0)

    def col(kind):
        return lambda r, j, c: (0, (kind * dil + r) * hg + j)

    q_spec = pl.BlockSpec((rc, LANES), lambda r, j, c: (c, r * hg + j))
    k_spec = pl.BlockSpec((rows, LANES), col(1))
    v_spec = pl.BlockSpec((rows, LANES), col(2))
    o_spec = pl.BlockSpec((rc, LANES), lambda r, j, c: (c, r * hg + j))
    shp = jax.ShapeDtypeStruct((rows, dil * gw), F32)
    return pl.pallas_call(
        kernel, grid=(dil, hg, rows // rc), in_specs=[q_spec, k_spec, v_spec],
        out_specs=[o_spec, o_spec], out_shape=[shp, shp],
        compiler_params=_params(("parallel", "parallel", "arbitrary")), name=f"banded_attention_g{g}",
    )(qkv_g, qkv_g, qkv_g)


def _mix_groups(outs, lses, cfg):
    m, gw = cfg.m, cfg.gw
    tm = _tile(m, 256)
    ng = len(outs)
    dils = [d for _, d in DIL_GROUPS]

    def kernel(*refs):
        o_refs, l_refs, out_ref = refs[:ng], refs[ng:2 * ng], refs[2 * ng]
        scr = list(refs[2 * ng + 1:])

        def token_major(ref, d):
            if d == 1:
                return ref[...]
            buf = scr.pop()
            for r in range(d):
                for h in range(cfg.hg):
                    lo = r * gw + h * LANES
                    buf[h, pl.ds(r, tm // d, stride=d), :] = ref[:, lo:lo + LANES]
            return jnp.concatenate([buf[h] for h in range(cfg.hg)], axis=1)

        os_ = [token_major(r, d) for r, d in zip(o_refs, dils)]
        ls = [token_major(r, d) for r, d in zip(l_refs, dils)]
        mx = functools.reduce(jnp.maximum, ls)
        es = [jnp.exp(l - mx) for l in ls]
        inv = 1.0 / functools.reduce(jnp.add, es)
        for gi in range(ng):
            out_ref[:, gi * gw:(gi + 1) * gw] = (os_[gi] * (es[gi] * inv)).astype(BF16)

    specs = [pl.BlockSpec((tm // d, d * gw), lambda i: (i, 0)) for d in dils]
    n_scr = 2 * sum(d > 1 for d in dils)
    return pl.pallas_call(
        kernel, grid=(m // tm,), in_specs=specs * 2,
        out_specs=pl.BlockSpec((tm, ng * gw), lambda i: (i, 0)),
        out_shape=jax.ShapeDtypeStruct((m, ng * gw), BF16),
        scratch_shapes=[pltpu.VMEM((cfg.hg, tm, LANES), F32)] * n_scr,
        compiler_params=_params(("parallel",)), name="mix_groups",
    )(*outs, *lses)


def _latent_attention(qt, kv, kpe, cfg):
    m, h = cfg.m, cfg.b_heads
    p_len, s_len = cfg.prompt_len, cfg.sample_len
    tq = _tile(s_len, 512)
    tk = _tile(s_len // 4, 512)
    assert s_len % (4 * tk) == 0 and p_len % (4 * tk) == 0, "each sequence needs >= 2 chunk pairs"

    nq = 2
    assert s_len % (nq * tq) == 0 and p_len % (nq * tq) == 0
    assert m % tk == 0

    def kernel(q_ref, kn_ref, kpe_ref, v_ref, o_ref, vt_ref, *scratch_refs):
        @pl.when(pl.program_id(1) == 0)
        def _():
            def transpose_chunk(c, carry):
                r0 = pl.multiple_of(c * tk, tk)
                vt_ref[:, pl.ds(r0, tk)] = v_ref[pl.ds(r0, tk), :].T
                return carry
            lax.fori_loop(0, m // tk, transpose_chunk, 0)

        row0 = pl.program_id(1) * (nq * tq)
        lo, n = _segment(row0, p_len, s_len)
        npair = n // (2 * tk)
        streams = [scratch_refs[5 * u:5 * u + 5] for u in range(nq)]
        qts = [q_ref[:, u * tq:(u + 1) * tq] for u in range(nq)]

        def scores(u, c, s_ref):
            ks = pl.multiple_of(lo + c * tk, tk)
            k = jnp.concatenate([kn_ref[pl.ds(ks, tk), :], kpe_ref[pl.ds(ks, tk), :][:, :ROPE_DIM]],
                                axis=1)
            s_ref[...] = jnp.dot(k, qts[u], preferred_element_type=F32)

        def softmax(s_ref, p_ref, mx):
            s = s_ref[...]
            mx_new = jnp.maximum(mx, jnp.max(s, axis=0, keepdims=True))
            p_ref[...] = jnp.exp2((s - mx_new).astype(BF16))
            return mx_new, jnp.exp2(mx - mx_new)

        ones_rows = jnp.ones((SUM_ROWS, tk), BF16)

        def values(c, p_ref, acc_ref, corr):
            ks = pl.multiple_of(lo + c * tk, tk)
            lhs = jnp.concatenate([vt_ref[:, pl.ds(ks, tk)], ones_rows], axis=0)
            acc_ref[...] = corr * acc_ref[...] + jnp.dot(lhs, p_ref[...], preferred_element_type=F32)

        def pair(i, carry, first=False, last=False):
            c = 2 * i
            out = []
            for u, (s_a, s_b, p_a, p_b, acc_ref) in enumerate(streams):
                mx, corr_prev = carry[u]
                scores(u, c + 1, s_b)
                mx, corr_a = softmax(s_a, p_a, mx)
                if not first:
                    values(c - 1, p_b, acc_ref, corr_prev)
                if not last:
                    scores(u, c + 2, s_a)
                mx, corr_b = softmax(s_b, p_b, mx)
                values(c, p_a, acc_ref, corr_a)
                out.append((mx, corr_b))
            return tuple(out)

        row = jnp.zeros((1, tq), F32)
        for u, (s_a, _, _, _, acc_ref) in enumerate(streams):
            acc_ref[...] = jnp.zeros_like(acc_ref)
            scores(u, 0, s_a)
        carry = pair(0, ((row - jnp.inf, row),) * nq, first=True)
        carry = lax.fori_loop(1, npair - 1, pair, carry)
        carry = pair(npair - 1, carry, last=True)
        for u, (_, _, _, p_b, acc_ref) in enumerate(streams):
            values(2 * npair - 1, p_b, acc_ref, carry[u][1])
            acc = acc_ref[...]
            o_ref[u * tq:(u + 1) * tq, :] = (acc[:V_DIM] / acc[V_DIM:V_DIM + 1]).T.astype(BF16)

    scratch = [pltpu.VMEM((V_DIM, m), BF16)]
    scratch += [pltpu.VMEM((tk, tq), F32), pltpu.VMEM((tk, tq), F32), pltpu.VMEM((tk, tq), BF16),
                pltpu.VMEM((tk, tq), BF16), pltpu.VMEM((V_DIM + SUM_ROWS, tq), F32)] * nq
    q_spec = pl.BlockSpec((QK_DIM, nq * tq), lambda hh, i: (hh, i))
    kn_spec = pl.BlockSpec((m, LANES), lambda hh, i: (0, 2 * hh))
    kpe_spec = pl.BlockSpec((m, LANES), lambda hh, i: (0, 0))
    v_spec = pl.BlockSpec((m, LANES), lambda hh, i: (0, 2 * hh + 1))
    o_spec = pl.BlockSpec((nq * tq, V_DIM), lambda hh, i: (i, hh))
    return pl.pallas_call(
        kernel, grid=(h, m // (nq * tq)), in_specs=[q_spec, kn_spec, kpe_spec, v_spec], out_specs=o_spec,
        out_shape=jax.ShapeDtypeStruct((m, h * V_DIM), BF16), scratch_shapes=scratch,
        compiler_params=_params(("parallel", "arbitrary")), name="latent_attention",
    )(qt, kv, kpe, kv)


def _rope_angles(pos, rot_dim):
    half = rot_dim // 2
    inv_freq = ROPE_THETA ** (-jnp.arange(half, dtype=F32) / half)
    return pos[:, None] * inv_freq[None, :]


def _rope_tables(pos, rot_dim):
    ang = _rope_angles(pos, rot_dim)
    cos, sin = jnp.cos(ang), jnp.sin(ang)
    rest = LANES - rot_dim
    ones = jnp.ones((pos.shape[0], rest), F32)
    zeros = jnp.zeros((pos.shape[0], rest), F32)
    zh = jnp.zeros_like(sin)
    c = jnp.concatenate([cos, cos, ones], axis=1)
    sl = jnp.concatenate([-sin, zh, zeros], axis=1)
    sr = jnp.concatenate([zh, sin, zeros], axis=1)
    return c, sl, sr


def _prep_weights(cfg, w_in, w_uq, w_ukv, w_ba, w_bb, w_o, w_gate, w_up, w_down):
    aw, ql, kl = cfg.a_width, cfg.q_lora, cfg.kv_lora
    off_cq = 3 * aw
    off_ga = off_cq + ql + kl + ROPE_DIM
    fpad = cfg.ff_pad - cfg.ff
    return dict(
        w_qkv=w_in[:, :off_cq].astype(BF16),
        w_lat=jnp.pad(w_in[:, off_cq:off_ga], ((0, 0), (0, LANES - ROPE_DIM))).astype(BF16),
        w_g=w_in[:, off_ga:].astype(BF16),
        w_uq_t=w_uq.T.astype(BF16), w_ukv=w_ukv.astype(BF16),
        w_ba=w_ba.astype(BF16), w_bb=w_bb.astype(BF16), w_o=w_o.astype(BF16),
        w_gate=jnp.pad(w_gate, ((0, 0), (0, fpad))).astype(BF16),
        w_up=jnp.pad(w_up, ((0, 0), (0, fpad))).astype(BF16),
        w_down=jnp.pad(w_down, ((0, fpad), (0, 0))).astype(BF16),
    )


def _layer(x, x_bf, w, q_norm, kv_norm, ln1_g, ln1_b, ln2_g, ln2_b, tabs_a, tabs_b, tabs_bt, cfg):
    c_q, c_kv, kpe = _proj_latent(x_bf, w["w_lat"], q_norm, kv_norm, tabs_b, cfg)
    gates = _proj_gates(x_bf, w["w_g"], cfg)

    outs, lses = [], []
    for g, (_, dil) in enumerate(DIL_GROUPS):
        qkv_g = _proj_qkv_group(x_bf, w["w_qkv"], tabs_a, g, dil, cfg)
        o, s = _banded_attention(qkv_g, g, dil, cfg)
        outs.append(o)
        lses.append(s)
    out_a = _mix_groups(outs, lses, cfg)

    qt = _proj_qt(c_q, w["w_uq_t"], *tabs_bt, cfg)
    kv = _proj_kv(c_kv, w["w_ukv"], cfg)
    out_b = _latent_attention(qt, kv, kpe, cfg)

    part_a = _branch_a(out_a, w["w_ba"], gates, cfg)
    merged = _branch_b_merge(out_b, w["w_bb"], gates, part_a, cfg)
    y1 = _residual_matmul(merged, w["w_o"], x, cfg, tm=1024, name="out_proj")
    x1, x1_bf = _layer_norm(y1, ln1_g, ln1_b, cfg)

    hmid = _swiglu(x1_bf, w["w_gate"], w["w_up"], cfg)
    y2 = _residual_matmul(hmid, w["w_down"], x1, cfg, tm=256, name="ffn_down")
    return _layer_norm(y2, ln2_g, ln2_b, cfg)


def _trunk(cfg, x_prompt, x_sample, w_in, mla_q_norm, w_uq, mla_kv_norm, w_ukv, w_branch_a, w_branch_b,
           w_out, ln1_g, ln1_b, w_ffn_gate, w_ffn_up, w_ffn_down, ln2_g, ln2_b):
    d = cfg.d_model
    x = jnp.concatenate([x_prompt.reshape(-1, d), x_sample.reshape(-1, d)], axis=0)
    x_bf = x.astype(BF16)
    pos = jnp.concatenate([jnp.arange(cfg.prompt_len, dtype=F32),
                           jnp.tile(jnp.arange(cfg.sample_len, dtype=F32), cfg.n_sample)])
    tabs_a = _rope_tables(pos, A_ROT_DIM)
    tabs_b = _rope_tables(pos, ROPE_DIM)
    ang_t = _rope_angles(pos, ROPE_DIM).T
    tabs_bt = (jnp.cos(ang_t), jnp.sin(ang_t))
    for l in range(cfg.depth):
        w = _prep_weights(cfg, w_in[l], w_uq[l], w_ukv[l], w_branch_a[l], w_branch_b[l], w_out[l],
                          w_ffn_gate[l], w_ffn_up[l], w_ffn_down[l])
        x, x_bf = _layer(x, x_bf, w, mla_q_norm[l], mla_kv_norm[l], ln1_g[l], ln1_b[l], ln2_g[l],
                         ln2_b[l], tabs_a, tabs_b, tabs_bt, cfg)
    return (x[:cfg.prompt_len].reshape(x_prompt.shape), x[cfg.prompt_len:].reshape(x_sample.shape))


def kernel(x_prompt, x_sample, w_in, mla_q_norm, w_uq, mla_kv_norm, w_ukv, w_branch_a, w_branch_b, w_out,
           ln1_g, ln1_b, w_ffn_gate, w_ffn_up, w_ffn_down, ln2_g, ln2_b):
    assert x_prompt.shape[0] == 1, "one prompt sequence"
    cfg = Cfg(d_model=x_prompt.shape[-1], prompt_len=x_prompt.shape[1], sample_len=x_sample.shape[1],
              n_sample=x_sample.shape[0], depth=w_in.shape[0],
              hg=w_branch_a.shape[1] // (len(DIL_GROUPS) * A_HEAD_DIM),
              b_heads=w_branch_b.shape[1] // V_DIM, q_lora=w_uq.shape[1], kv_lora=w_ukv.shape[1],
              ff=w_ffn_gate.shape[2])
    return _trunk(cfg, x_prompt, x_sample, w_in, mla_q_norm, w_uq, mla_kv_norm, w_ukv, w_branch_a,
                  w_branch_b, w_out, ln1_g, ln1_b, w_ffn_gate, w_ffn_up, w_ffn_down, ln2_g, ln2_b)
```

```python
import functools
from typing import NamedTuple

import jax
import jax.numpy as jnp
from jax import lax
from jax.experimental import pallas as pl
from jax.experimental.pallas import tpu as pltpu

F32 = jnp.float32
BF16 = jnp.bfloat16

LANES = 128
BF16_ROWS = 16
ROPE_THETA = 500000.0
LN_EPS = 1e-5
RMS_EPS = 1e-6
NEG_BIG = -1e30
LOG2_E = 1.4426950408889634

DIL_GROUPS = ((128, 1), (512, 4), (2048, 16))
BAND_W = 64
BAND_UNROLL = 8
A_HEAD_DIM = 128
A_ROT_DIM = 32
NOPE_DIM = 128
ROPE_DIM = 64
V_DIM = 128
QK_DIM = NOPE_DIM + ROPE_DIM
SUM_ROWS = BF16_ROWS
COL_SLAB = 256
VMEM_LIMIT_MB = 56


class Cfg(NamedTuple):
    d_model: int
    prompt_len: int
    sample_len: int
    n_sample: int
    depth: int
    hg: int
    b_heads: int
    q_lora: int
    kv_lora: int
    ff: int

    @property
    def m(self):
        return self.prompt_len + self.n_sample * self.sample_len

    @property
    def gw(self):
        return self.hg * A_HEAD_DIM

    @property
    def a_width(self):
        return len(DIL_GROUPS) * self.gw

    @property
    def ff_pad(self):
        return -(-self.ff // 1024) * 1024 if self.ff > 1024 else -(-self.ff // LANES) * LANES

    @property
    def alpha(self):
        return (2.0 * self.depth) ** 0.25


def _tile(n, pref):
    if n <= pref:
        return n
    t = (pref // LANES) * LANES
    while n % t:
        t -= LANES
    return t


def _params(sem):
    return pltpu.CompilerParams(dimension_semantics=sem,
                                vmem_limit_bytes=VMEM_LIMIT_MB * 1024 * 1024)


def _matmul(a, bs, extras, out_defs, epilogue, *, tm, tn, name, n_col_blocks=None, b_block=None,
            scratch=(), col_split=1):
    m, k_dim = a.shape
    nj = bs[0].shape[1] // tn if n_col_blocks is None else n_col_blocks
    b_block = (lambda j: j) if b_block is None else b_block
    nb, ne, no = len(bs), len(extras), len(out_defs)
    swap = lambda f: (lambda j, i: f(i, j))

    a_spec = pl.BlockSpec((tm, k_dim), lambda j, i: (i, 0))
    b_spec = pl.BlockSpec((k_dim, tn), lambda j, i: (0, b_block(j)))
    in_specs = [a_spec] + [b_spec] * nb + [pl.BlockSpec(bshape, swap(f)) for _, bshape, f in extras]
    out_specs = [pl.BlockSpec(d[2], swap(d[3] if len(d) > 3 else (lambda i, j: (i, j)))) for d in out_defs]
    out_shape = [jax.ShapeDtypeStruct(d[0], d[1]) for d in out_defs]

    def kernel(*refs):
        a_ref = refs[0]
        b_refs = refs[1:1 + nb]
        e_refs = refs[1 + nb:1 + nb + ne]
        o_refs = refs[1 + nb + ne:1 + nb + ne + no]
        s_refs = refs[1 + nb + ne + no:]
        width = tn // col_split
        for s in range(col_split):
            cs = slice(s * width, (s + 1) * width)
            accs = [jnp.dot(a_ref[...], b[:, cs], preferred_element_type=F32) for b in b_refs]
            epilogue(accs, e_refs, o_refs, pl.program_id(0), s_refs, cs)

    return pl.pallas_call(
        kernel, grid=(nj, m // tm), in_specs=in_specs, out_specs=out_specs, out_shape=out_shape,
        scratch_shapes=list(scratch), compiler_params=_params(("parallel", "arbitrary")), name=name,
    )(a, *bs, *[e[0] for e in extras])


def _rows_out(m, n, dtype, tm, tn):
    return ((m, n), dtype, (tm, tn))


def _rope_lanes(x, c, sl, sr, half):
    return x * c + pltpu.roll(x, LANES - half, 1) * sl + pltpu.roll(x, half, 1) * sr


def _row_tables(tabs, tm):
    return [(t, (tm, LANES), lambda i, j: (i, 0)) for t in tabs]


def _proj_qkv_group(x_bf, w_qkv, tabs_a, g, dil, cfg):
    m, gw = cfg.m, cfg.gw
    tm = _tile(m, 1024)
    half = A_ROT_DIM // 2
    ngroups = len(DIL_GROUPS)

    def epilogue(accs, e, o, j, scr, cs):
        acc = accs[0]
        h0 = cs.start // LANES
        local = [slice(h * LANES, (h + 1) * LANES) for h in range((cs.stop - cs.start) // LANES)]

        def emit(piece):
            if dil == 1:
                for h, hs in enumerate(local):
                    lo = (h0 + h) * LANES
                    o[0][:, lo:lo + LANES] = piece(hs).astype(BF16)
                return
            for h, hs in enumerate(local):
                scr[0][h0 + h] = piece(hs)
            for r in range(dil):
                for h in range(len(local)):
                    lo = r * gw + (h0 + h) * LANES
                    o[0][:, lo:lo + LANES] = scr[0][h0 + h, pl.ds(r, tm // dil, stride=dil), :].astype(BF16)

        rot = j < 2
        c = jnp.where(rot, e[0][...], 1.0)
        sl = jnp.where(rot, e[1][...], 0.0)
        sr = jnp.where(rot, e[2][...], 0.0)
        emit(lambda hs: _rope_lanes(acc[:, hs], c, sl, sr, half))

    out_def = ((m // dil, 3 * dil * gw), BF16, (tm // dil, dil * gw))
    scratch = [pltpu.VMEM((gw // LANES, tm, LANES), F32)] if dil > 1 else []
    return _matmul(x_bf, [w_qkv], _row_tables(tabs_a, tm), [out_def], epilogue, tm=tm, tn=gw,
                   n_col_blocks=3, b_block=lambda j: j * ngroups + g, scratch=scratch,
                   col_split=max(1, gw // COL_SLAB), name=f"proj_qkv_g{g}")[0]


def _proj_latent(x_bf, w_lat, q_norm, kv_norm, tabs_b, cfg):
    m, ql, kl = cfg.m, cfg.q_lora, cfg.kv_lora
    n = ql + kl + LANES
    tm = _tile(m, 512)
    half = ROPE_DIM // 2

    def rms(v, g):
        return v * lax.rsqrt(jnp.mean(v * v, axis=-1, keepdims=True) + RMS_EPS) * g

    def epilogue(accs, e, o, j, scr, cs):
        acc = accs[0]
        o[0][...] = rms(acc[:, :ql], e[0][...]).astype(BF16)
        o[1][...] = rms(acc[:, ql:ql + kl], e[1][...]).astype(BF16)
        o[2][...] = _rope_lanes(acc[:, ql + kl:], e[2][...], e[3][...], e[4][...], half).astype(BF16)

    extras = [(q_norm.reshape(1, ql), (1, ql), lambda i, j: (0, 0)),
              (kv_norm.reshape(1, kl), (1, kl), lambda i, j: (0, 0))] + _row_tables(tabs_b, tm)
    outs = [_rows_out(m, ql, BF16, tm, ql), _rows_out(m, kl, BF16, tm, kl),
            _rows_out(m, LANES, BF16, tm, LANES)]
    return _matmul(x_bf, [w_lat], extras, outs, epilogue, tm=tm, tn=n, name="proj_latent")


def _proj_gates(x_bf, w_g, cfg):
    tm, tn = _tile(cfg.m, 1024), _tile(2 * cfg.d_model, 1024)

    def epilogue(accs, e, o, j, scr, cs):
        o[0][:, cs] = jax.nn.sigmoid(accs[0]).astype(BF16)

    return _matmul(x_bf, [w_g], [], [_rows_out(cfg.m, 2 * cfg.d_model, BF16, tm, tn)], epilogue,
                   tm=tm, tn=tn, col_split=tn // COL_SLAB, name="proj_gates")[0]


def _proj_qt(c_q, w_uq_t, cos_t, sin_t, cfg):
    m, ql = c_q.shape
    n = w_uq_t.shape[0]
    heads_per_tile = min(4, cfg.b_heads)
    tn = heads_per_tile * QK_DIM
    tm = _tile(m, 1024)
    half = ROPE_DIM // 2
    scale = QK_DIM ** -0.5 * LOG2_E

    def kernel(w_ref, c_ref, cos_ref, sin_ref, o_ref):
        acc = lax.dot_general(w_ref[...], c_ref[...], (((1,), (1,)), ((), ())),
                              preferred_element_type=F32) * scale
        cos, sin = cos_ref[...], sin_ref[...]
        for h in range(heads_per_tile):
            lo = h * QK_DIM
            o_ref[lo:lo + NOPE_DIM, :] = acc[lo:lo + NOPE_DIM, :].astype(BF16)
            x1 = acc[lo + NOPE_DIM:lo + NOPE_DIM + half, :]
            x2 = acc[lo + NOPE_DIM + half:lo + QK_DIM, :]
            o_ref[lo + NOPE_DIM:lo + NOPE_DIM + half, :] = (x1 * cos - x2 * sin).astype(BF16)
            o_ref[lo + NOPE_DIM + half:lo + QK_DIM, :] = (x1 * sin + x2 * cos).astype(BF16)

    tab = pl.BlockSpec((half, tm), lambda j, i: (0, i))
    return pl.pallas_call(
        kernel, grid=(n // tn, m // tm),
        in_specs=[pl.BlockSpec((tn, ql), lambda j, i: (j, 0)), pl.BlockSpec((tm, ql), lambda j, i: (i, 0)),
                  tab, tab],
        out_specs=pl.BlockSpec((tn, tm), lambda j, i: (j, i)),
        out_shape=jax.ShapeDtypeStruct((n, m), BF16),
        compiler_params=_params(("parallel", "arbitrary")), name="proj_qt",
    )(w_uq_t, c_q, cos_t, sin_t)


def _proj_kv(c_kv, w_ukv, cfg):
    m = cfg.m
    n = cfg.b_heads * (NOPE_DIM + V_DIM)
    tm, tn = _tile(m, 1024), _tile(n, 1024)
    slabs = tn // LANES

    def epilogue(accs, e, o, j, scr, cs):
        for s in range(slabs):
            o[0][s] = accs[0][:, s * LANES:(s + 1) * LANES].astype(BF16)

    out_def = ((n // LANES, m, LANES), BF16, (slabs, tm, LANES), lambda i, j: (j, i, 0))
    return _matmul(c_kv, [w_ukv], [], [out_def], epilogue, tm=tm, tn=tn, name="proj_kv")[0]


def _branch_a(out_a, w_ba, gates, cfg):
    tm, tn = _tile(cfg.m, 1024), _tile(cfg.d_model, 512)

    def epilogue(accs, e, o, j, scr, cs):
        o[0][:, cs] = e[0][:, cs].astype(F32) * accs[0]

    extras = [(gates, (tm, tn), lambda i, j: (i, j))]
    return _matmul(out_a, [w_ba], extras, [_rows_out(cfg.m, cfg.d_model, F32, tm, tn)], epilogue,
                   tm=tm, tn=tn, col_split=tn // COL_SLAB, name="branch_a")[0]


def _branch_b_merge(out_b, w_bb, gates, part_a, cfg):
    tm, tn = _tile(cfg.m, 512), _tile(cfg.d_model, 512)
    goff = cfg.d_model // tn

    def epilogue(accs, e, o, j, scr, cs):
        o[0][:, cs] = (e[1][:, cs] + e[0][:, cs].astype(F32) * accs[0]).astype(BF16)

    extras = [(gates, (tm, tn), lambda i, j: (i, j + goff)),
              (part_a, (tm, tn), lambda i, j: (i, j))]
    return _matmul(out_b, [w_bb], extras, [_rows_out(cfg.m, cfg.d_model, BF16, tm, tn)], epilogue,
                   tm=tm, tn=tn, col_split=tn // COL_SLAB, name="branch_b_merge")[0]


def _residual_matmul(a, w, x_res, cfg, *, tm, name):
    tm, tn = _tile(cfg.m, tm), _tile(cfg.d_model, 512)
    alpha = cfg.alpha

    def epilogue(accs, e, o, j, scr, cs):
        o[0][:, cs] = alpha * e[0][:, cs] + accs[0]

    extras = [(x_res, (tm, tn), lambda i, j: (i, j))]
    return _matmul(a, [w], extras, [_rows_out(cfg.m, cfg.d_model, F32, tm, tn)], epilogue,
                   tm=tm, tn=tn, col_split=tn // COL_SLAB, name=name)[0]


def _swiglu(x_bf, w_gate, w_up, cfg):
    tm, tn = _tile(cfg.m, 1024), _tile(cfg.ff_pad, 512)

    def epilogue(accs, e, o, j, scr, cs):
        g, u = accs
        o[0][:, cs] = (g * jax.nn.sigmoid(g) * u).astype(BF16)

    return _matmul(x_bf, [w_gate, w_up], [], [_rows_out(cfg.m, cfg.ff_pad, BF16, tm, tn)], epilogue,
                   tm=tm, tn=tn, col_split=tn // COL_SLAB, name="swiglu")[0]


def _layer_norm(y, g, b, cfg):
    m, d = y.shape
    tm = _tile(m, 256)

    def kernel(y_ref, g_ref, b_ref, o_ref, obf_ref):
        v = y_ref[...]
        mu = jnp.mean(v, axis=-1, keepdims=True)
        vc = v - mu
        var = jnp.mean(vc * vc, axis=-1, keepdims=True)
        out = vc * lax.rsqrt(var + LN_EPS) * g_ref[...] + b_ref[...]
        o_ref[...] = out
        obf_ref[...] = out.astype(BF16)

    row = pl.BlockSpec((tm, d), lambda i: (i, 0))
    vec = pl.BlockSpec((1, d), lambda i: (0, 0))
    return pl.pallas_call(
        kernel, grid=(m // tm,), in_specs=[row, vec, vec], out_specs=[row, row],
        out_shape=[jax.ShapeDtypeStruct((m, d), F32), jax.ShapeDtypeStruct((m, d), BF16)],
        compiler_params=_params(("parallel",)), name="layer_norm",
    )(y, g.reshape(1, d), b.reshape(1, d))


def _final_layer_norm(y, g, b, cfg):
    m, d = y.shape
    tm = _tile(cfg.sample_len, 256)
    p_blocks = cfg.prompt_len // tm

    def kernel(y_ref, g_ref, b_ref, op_ref, os_ref):
        v = y_ref[...]
        mu = jnp.mean(v, axis=-1, keepdims=True)
        vc = v - mu
        var = jnp.mean(vc * vc, axis=-1, keepdims=True)
        out = vc * lax.rsqrt(var + LN_EPS) * g_ref[...] + b_ref[...]
        i = pl.program_id(0)

        @pl.when(i < p_blocks)
        def _():
            op_ref[...] = out

        @pl.when(i >= p_blocks)
        def _():
            os_ref[...] = out

    row = pl.BlockSpec((tm, d), lambda i: (i, 0))
    vec = pl.BlockSpec((1, d), lambda i: (0, 0))
    p_spec = pl.BlockSpec((tm, d), lambda i: (jnp.minimum(i, p_blocks - 1), 0))
    s_spec = pl.BlockSpec((tm, d), lambda i: (jnp.maximum(i - p_blocks, 0), 0))
    return pl.pallas_call(
        kernel, grid=(m // tm,), in_specs=[row, vec, vec], out_specs=[p_spec, s_spec],
        out_shape=[jax.ShapeDtypeStruct((cfg.prompt_len, d), F32),
                   jax.ShapeDtypeStruct((m - cfg.prompt_len, d), F32)],
        compiler_params=_params(("arbitrary",)), name="final_layer_norm",
    )(y, g.reshape(1, d), b.reshape(1, d))


def _segment(row0, p_len, s_len):
    in_prompt = row0 < p_len
    s_idx = jnp.maximum(row0 - p_len, 0) // s_len
    lo = jnp.where(in_prompt, 0, p_len + s_idx * s_len)
    return lo, jnp.where(in_prompt, p_len, s_len)


def _banded_attention(qkv_g, g, dil, cfg):
    hg, gw = cfg.hg, cfg.gw
    rows = cfg.m // dil
    p_len, s_len = cfg.prompt_len // dil, cfg.sample_len // dil
    bq = 2 * BAND_W
    kw = bq + 2 * BAND_W
    rc = _tile(rows, 2048)
    nsub = rc // bq
    scale = A_HEAD_DIM ** -0.5

    def kernel(q_ref, k_ref, v_ref, o_ref, lse_ref):
        base = pl.program_id(2) * rc

        def body(t, carry):
            s0 = pl.multiple_of(t * bq, bq)
            g0 = base + s0
            ks = pl.multiple_of(jnp.clip(g0 - BAND_W, 0, rows - kw), BAND_W)
            q = q_ref[pl.ds(s0, bq), :]
            k = k_ref[pl.ds(ks, kw), :]
            v = v_ref[pl.ds(ks, kw), :]
            s = lax.dot_general(q, k, (((1,), (1,)), ((), ())), preferred_element_type=F32) * scale
            qrow = g0 + lax.broadcasted_iota(jnp.int32, (bq, kw), 0)
            krow = ks + lax.broadcasted_iota(jnp.int32, (bq, kw), 1)
            lo, n = _segment(g0, p_len, s_len)
            valid = (jnp.abs(krow - qrow) <= BAND_W) & (krow >= lo) & (krow < lo + n)
            s = jnp.where(valid, s, NEG_BIG)
            mx = jnp.max(s, axis=-1, keepdims=True)
            p = jnp.exp(s - mx)
            l = jnp.sum(p, axis=-1, keepdims=True)
            pv = jnp.dot(p.astype(BF16), v, preferred_element_type=F32)
            o_ref[pl.ds(s0, bq), :] = pv / l
            lse_ref[pl.ds(s0, bq), :] = jnp.broadcast_to(mx + jnp.log(l), (bq, LANES))
            return carry

        lax.fori_loop(0, nsub, body, 0, unroll=BAND_UNROLL)

    def col(kind):
        return lambda r, j, c: (0, (kind * dil + r) * hg + j)

    q_spec = pl.BlockSpec((rc, LANES), lambda r, j, c: (c, r * hg + j))
    k_spec = pl.BlockSpec((rows, LANES), col(1))
    v_spec = pl.BlockSpec((rows, LANES), col(2))
    o_spec = pl.BlockSpec((rc, LANES), lambda r, j, c: (c, r * hg + j))
    shp = jax.ShapeDtypeStruct((rows, dil * gw), F32)
    return pl.pallas_call(
        kernel, grid=(dil, hg, rows // rc), in_specs=[q_spec, k_spec, v_spec],
        out_specs=[o_spec, o_spec], out_shape=[shp, shp],
        compiler_params=_params(("parallel", "parallel", "arbitrary")), name=f"banded_attention_g{g}",
    )(qkv_g, qkv_g, qkv_g)


def _mix_groups(outs, lses, cfg):
    m, gw = cfg.m, cfg.gw
    tm = _tile(m, 256)
    ng = len(outs)
    dils = [d for _, d in DIL_GROUPS]

    def kernel(*refs):
        o_refs, l_refs, out_ref = refs[:ng], refs[ng:2 * ng], refs[2 * ng]
        scr = list(refs[2 * ng + 1:])

        def token_major(ref, d):
            if d == 1:
                return ref[...]
            buf = scr.pop()
            for r in range(d):
                for h in range(cfg.hg):
                    lo = r * gw + h * LANES
                    buf[h, pl.ds(r, tm // d, stride=d), :] = ref[:, lo:lo + LANES]
            return jnp.concatenate([buf[h] for h in range(cfg.hg)], axis=1)

        os_ = [token_major(r, d) for r, d in zip(o_refs, dils)]
        ls = [token_major(r, d) for r, d in zip(l_refs, dils)]
        mx = functools.reduce(jnp.maximum, ls)
        es = [jnp.exp(l - mx) for l in ls]
        inv = 1.0 / functools.reduce(jnp.add, es)
        for gi in range(ng):
            out_ref[:, gi * gw:(gi + 1) * gw] = (os_[gi] * (es[gi] * inv)).astype(BF16)

    specs = [pl.BlockSpec((tm // d, d * gw), lambda i: (i, 0)) for d in dils]
    n_scr = 2 * sum(d > 1 for d in dils)
    return pl.pallas_call(
        kernel, grid=(m // tm,), in_specs=specs * 2,
        out_specs=pl.BlockSpec((tm, ng * gw), lambda i: (i, 0)),
        out_shape=jax.ShapeDtypeStruct((m, ng * gw), BF16),
        scratch_shapes=[pltpu.VMEM((cfg.hg, tm, LANES), F32)] * n_scr,
        compiler_params=_params(("parallel",)), name="mix_groups",
    )(*outs, *lses)


def _latent_attention(qt, kv, kpe, cfg):
    m, h = cfg.m, cfg.b_heads
    p_len, s_len = cfg.prompt_len, cfg.sample_len
    tq = _tile(s_len, 512)
    tk = _tile(s_len // 4, 512)
    assert s_len % (4 * tk) == 0 and p_len % (4 * tk) == 0, "each sequence needs >= 2 chunk pairs"

    nq = 2
    assert s_len % (nq * tq) == 0 and p_len % (nq * tq) == 0
    assert m % tk == 0

    def kernel(q_ref, kn_ref, kpe_ref, v_ref, o_ref, vt_ref, *scratch_refs):
        @pl.when(pl.program_id(1) == 0)
        def _():
            def transpose_chunk(c, carry):
                r0 = pl.multiple_of(c * tk, tk)
                vt_ref[:, pl.ds(r0, tk)] = v_ref[pl.ds(r0, tk), :].T
                return carry
            lax.fori_loop(0, m // tk, transpose_chunk, 0)

        row0 = pl.program_id(1) * (nq * tq)
        lo, n = _segment(row0, p_len, s_len)
        npair = n // (2 * tk)
        streams = [scratch_refs[5 * u:5 * u + 5] for u in range(nq)]
        qts = [q_ref[:, u * tq:(u + 1) * tq] for u in range(nq)]

        def scores(u, c, s_ref):
            ks = pl.multiple_of(lo + c * tk, tk)
            k = jnp.concatenate([kn_ref[pl.ds(ks, tk), :], kpe_ref[pl.ds(ks, tk), :][:, :ROPE_DIM]],
                                axis=1)
            s = jnp.dot(k, qts[u], preferred_element_type=F32)
            s_ref[...] = s
            return jnp.max(s, axis=0, keepdims=True)

        def softmax(s_ref, p_ref, mx, chunk_max):
            mx_new = jnp.maximum(mx, chunk_max)
            p_ref[...] = jnp.exp2((s_ref[...] - mx_new).astype(BF16))
            return mx_new, jnp.exp2(mx - mx_new)

        ones_rows = jnp.ones((SUM_ROWS, tk), BF16)

        def values(c, p_ref, acc_ref, corr):
            ks = pl.multiple_of(lo + c * tk, tk)
            lhs = jnp.concatenate([vt_ref[:, pl.ds(ks, tk)], ones_rows], axis=0)
            acc_ref[...] = corr * acc_ref[...] + jnp.dot(lhs, p_ref[...], preferred_element_type=F32)

        def pair(i, carry, first=False, last=False):
            c = 2 * i
            out = []
            for u, (s_a, s_b, p_a, p_b, acc_ref) in enumerate(streams):
                mx, max_a, corr_prev = carry[u]
                max_b = scores(u, c + 1, s_b)
                mx, corr_a = softmax(s_a, p_a, mx, max_a)
                if not first:
                    values(c - 1, p_b, acc_ref, corr_prev)
                if not last:
                    max_a = scores(u, c + 2, s_a)
                mx, corr_b = softmax(s_b, p_b, mx, max_b)
                values(c, p_a, acc_ref, corr_a)
                out.append((mx, max_a, corr_b))
            return tuple(out)

        row = jnp.zeros((1, tq), F32)
        init = []
        for u, (s_a, _, _, _, acc_ref) in enumerate(streams):
            acc_ref[...] = jnp.zeros_like(acc_ref)
            init.append((row - jnp.inf, scores(u, 0, s_a), row))
        carry = pair(0, tuple(init), first=True)
        carry = lax.fori_loop(1, npair - 1, pair, carry)
        carry = pair(npair - 1, carry, last=True)
        for u, (_, _, _, p_b, acc_ref) in enumerate(streams):
            values(2 * npair - 1, p_b, acc_ref, carry[u][2])
            acc = acc_ref[...]
            o_ref[u * tq:(u + 1) * tq, :] = (acc[:V_DIM] / acc[V_DIM:V_DIM + 1]).T.astype(BF16)

    scratch = [pltpu.VMEM((V_DIM, m), BF16)]
    scratch += [pltpu.VMEM((tk, tq), F32), pltpu.VMEM((tk, tq), F32), pltpu.VMEM((tk, tq), BF16),
                pltpu.VMEM((tk, tq), BF16), pltpu.VMEM((V_DIM + SUM_ROWS, tq), F32)] * nq
    q_spec = pl.BlockSpec((QK_DIM, nq * tq), lambda hh, i: (hh, i))
    kn_spec = pl.BlockSpec((None, m, LANES), lambda hh, i: (2 * hh, 0, 0))
    kpe_spec = pl.BlockSpec((m, LANES), lambda hh, i: (0, 0))
    v_spec = pl.BlockSpec((None, m, LANES), lambda hh, i: (2 * hh + 1, 0, 0))
    o_spec = pl.BlockSpec((nq * tq, V_DIM), lambda hh, i: (i, hh))
    return pl.pallas_call(
        kernel, grid=(h, m // (nq * tq)), in_specs=[q_spec, kn_spec, kpe_spec, v_spec], out_specs=o_spec,
        out_shape=jax.ShapeDtypeStruct((m, h * V_DIM), BF16), scratch_shapes=scratch,
        compiler_params=_params(("parallel", "arbitrary")), name="latent_attention",
    )(qt, kv, kpe, kv)


def _rope_angles(pos, rot_dim):
    half = rot_dim // 2
    inv_freq = ROPE_THETA ** (-jnp.arange(half, dtype=F32) / half)
    return pos[:, None] * inv_freq[None, :]


def _rope_tables(pos, rot_dim):
    ang = _rope_angles(pos, rot_dim)
    cos, sin = jnp.cos(ang), jnp.sin(ang)
    rest = LANES - rot_dim
    ones = jnp.ones((pos.shape[0], rest), F32)
    zeros = jnp.zeros((pos.shape[0], rest), F32)
    zh = jnp.zeros_like(sin)
    c = jnp.concatenate([cos, cos, ones], axis=1)
    sl = jnp.concatenate([-sin, zh, zeros], axis=1)
    sr = jnp.concatenate([zh, sin, zeros], axis=1)
    return c, sl, sr


def _prep_weights(cfg, w_in, w_uq, w_ukv, w_ba, w_bb, w_o, w_gate, w_up, w_down):
    aw, ql, kl = cfg.a_width, cfg.q_lora, cfg.kv_lora
    off_cq = 3 * aw
    off_ga = off_cq + ql + kl + ROPE_DIM
    fpad = cfg.ff_pad - cfg.ff
    return dict(
        w_qkv=w_in[:, :off_cq].astype(BF16),
        w_lat=jnp.pad(w_in[:, off_cq:off_ga], ((0, 0), (0, LANES - ROPE_DIM))).astype(BF16),
        w_g=w_in[:, off_ga:].astype(BF16),
        w_uq_t=w_uq.T.astype(BF16), w_ukv=w_ukv.astype(BF16),
        w_ba=w_ba.astype(BF16), w_bb=w_bb.astype(BF16), w_o=w_o.astype(BF16),
        w_gate=jnp.pad(w_gate, ((0, 0), (0, fpad))).astype(BF16),
        w_up=jnp.pad(w_up, ((0, 0), (0, fpad))).astype(BF16),
        w_down=jnp.pad(w_down, ((0, fpad), (0, 0))).astype(BF16),
    )


def _layer(x, x_bf, w, q_norm, kv_norm, ln1_g, ln1_b, ln2_g, ln2_b, tabs_a, tabs_b, tabs_bt, cfg, last):
    c_q, c_kv, kpe = _proj_latent(x_bf, w["w_lat"], q_norm, kv_norm, tabs_b, cfg)
    gates = _proj_gates(x_bf, w["w_g"], cfg)

    outs, lses = [], []
    for g, (_, dil) in enumerate(DIL_GROUPS):
        qkv_g = _proj_qkv_group(x_bf, w["w_qkv"], tabs_a, g, dil, cfg)
        o, s = _banded_attention(qkv_g, g, dil, cfg)
        outs.append(o)
        lses.append(s)
    out_a = _mix_groups(outs, lses, cfg)

    qt = _proj_qt(c_q, w["w_uq_t"], *tabs_bt, cfg)
    kv = _proj_kv(c_kv, w["w_ukv"], cfg)
    out_b = _latent_attention(qt, kv, kpe, cfg)

    part_a = _branch_a(out_a, w["w_ba"], gates, cfg)
    merged = _branch_b_merge(out_b, w["w_bb"], gates, part_a, cfg)
    y1 = _residual_matmul(merged, w["w_o"], x, cfg, tm=1024, name="out_proj")
    x1, x1_bf = _layer_norm(y1, ln1_g, ln1_b, cfg)

    hmid = _swiglu(x1_bf, w["w_gate"], w["w_up"], cfg)
    y2 = _residual_matmul(hmid, w["w_down"], x1, cfg, tm=256, name="ffn_down")
    return (_final_layer_norm if last else _layer_norm)(y2, ln2_g, ln2_b, cfg)


def _trunk(cfg, x_prompt, x_sample, w_in, mla_q_norm, w_uq, mla_kv_norm, w_ukv, w_branch_a, w_branch_b,
           w_out, ln1_g, ln1_b, w_ffn_gate, w_ffn_up, w_ffn_down, ln2_g, ln2_b):
    d = cfg.d_model
    x = jnp.concatenate([x_prompt.reshape(-1, d), x_sample.reshape(-1, d)], axis=0)
    x_bf = x.astype(BF16)
    pos = jnp.concatenate([jnp.arange(cfg.prompt_len, dtype=F32),
                           jnp.tile(jnp.arange(cfg.sample_len, dtype=F32), cfg.n_sample)])
    tabs_a = _rope_tables(pos, A_ROT_DIM)
    tabs_b = _rope_tables(pos, ROPE_DIM)
    ang_t = _rope_angles(pos, ROPE_DIM).T
    tabs_bt = (jnp.cos(ang_t), jnp.sin(ang_t))
    for l in range(cfg.depth):
        w = _prep_weights(cfg, w_in[l], w_uq[l], w_ukv[l], w_branch_a[l], w_branch_b[l], w_out[l],
                          w_ffn_gate[l], w_ffn_up[l], w_ffn_down[l])
        last = l == cfg.depth - 1
        out = _layer(x, x_bf, w, mla_q_norm[l], mla_kv_norm[l], ln1_g[l], ln1_b[l], ln2_g[l],
                     ln2_b[l], tabs_a, tabs_b, tabs_bt, cfg, last)
        if not last:
            x, x_bf = out
    y_prompt, y_sample = out
    return (y_prompt.reshape(x_prompt.shape), y_sample.reshape(x_sample.shape))


def kernel(x_prompt, x_sample, w_in, mla_q_norm, w_uq, mla_kv_norm, w_ukv, w_branch_a, w_branch_b, w_out,
           ln1_g, ln1_b, w_ffn_gate, w_ffn_up, w_ffn_down, ln2_g, ln2_b):
    assert x_prompt.shape[0] == 1, "one prompt sequence"
    cfg = Cfg(d_model=x_prompt.shape[-1], prompt_len=x_prompt.shape[1], sample_len=x_sample.shape[1],
              n_sample=x_sample.shape[0], depth=w_in.shape[0],
              hg=w_branch_a.shape[1] // (len(DIL_GROUPS) * A_HEAD_DIM),
              b_heads=w_branch_b.shape[1] // V_DIM, q_lora=w_uq.shape[1], kv_lora=w_ukv.shape[1],
              ff=w_ffn_gate.shape[2])
    return _trunk(cfg, x_prompt, x_sample, w_in, mla_q_norm, w_uq, mla_kv_norm, w_ukv, w_branch_a,
                  w_branch_b, w_out, ln1_g, ln1_b, w_ffn_gate, w_ffn_up, w_ffn_down, ln2_g, ln2_b)
```

```python
import functools
from typing import NamedTuple

import jax
import jax.numpy as jnp
from jax import lax
from jax.experimental import pallas as pl
from jax.experimental.pallas import tpu as pltpu

F32 = jnp.float32
BF16 = jnp.bfloat16

LANES = 128
BF16_ROWS = 16
ROPE_THETA = 500000.0
LN_EPS = 1e-5
RMS_EPS = 1e-6
NEG_BIG = -1e30
LOG2_E = 1.4426950408889634

DIL_GROUPS = ((128, 1), (512, 4), (2048, 16))
BAND_W = 64
BAND_UNROLL = 8
A_HEAD_DIM = 128
A_ROT_DIM = 32
NOPE_DIM = 128
ROPE_DIM = 64
V_DIM = 128
QK_DIM = NOPE_DIM + ROPE_DIM
SUM_ROWS = BF16_ROWS
COL_SLAB = 256
VMEM_LIMIT_MB = 56


class Cfg(NamedTuple):
    d_model: int
    prompt_len: int
    sample_len: int
    n_sample: int
    depth: int
    hg: int
    b_heads: int
    q_lora: int
    kv_lora: int
    ff: int

    @property
    def m(self):
        return self.prompt_len + self.n_sample * self.sample_len

    @property
    def gw(self):
        return self.hg * A_HEAD_DIM

    @property
    def a_width(self):
        return len(DIL_GROUPS) * self.gw

    @property
    def ff_pad(self):
        return -(-self.ff // 1024) * 1024 if self.ff > 1024 else -(-self.ff // LANES) * LANES

    @property
    def alpha(self):
        return (2.0 * self.depth) ** 0.25


def _tile(n, pref):
    if n <= pref:
        return n
    t = (pref // LANES) * LANES
    while n % t:
        t -= LANES
    return t


def _params(sem):
    return pltpu.CompilerParams(dimension_semantics=sem,
                                vmem_limit_bytes=VMEM_LIMIT_MB * 1024 * 1024)


def _matmul(a, bs, extras, out_defs, epilogue, *, tm, tn, name, n_col_blocks=None, b_block=None,
            scratch=(), col_split=1):
    m, k_dim = a.shape
    nj = bs[0].shape[1] // tn if n_col_blocks is None else n_col_blocks
    b_block = (lambda j: j) if b_block is None else b_block
    nb, ne, no = len(bs), len(extras), len(out_defs)
    swap = lambda f: (lambda j, i: f(i, j))

    a_spec = pl.BlockSpec((tm, k_dim), lambda j, i: (i, 0))
    b_spec = pl.BlockSpec((k_dim, tn), lambda j, i: (0, b_block(j)))
    in_specs = [a_spec] + [b_spec] * nb + [pl.BlockSpec(bshape, swap(f)) for _, bshape, f in extras]
    out_specs = [pl.BlockSpec(d[2], swap(d[3] if len(d) > 3 else (lambda i, j: (i, j)))) for d in out_defs]
    out_shape = [jax.ShapeDtypeStruct(d[0], d[1]) for d in out_defs]

    def kernel(*refs):
        a_ref = refs[0]
        b_refs = refs[1:1 + nb]
        e_refs = refs[1 + nb:1 + nb + ne]
        o_refs = refs[1 + nb + ne:1 + nb + ne + no]
        s_refs = refs[1 + nb + ne + no:]
        width = tn // col_split
        for s in range(col_split):
            cs = slice(s * width, (s + 1) * width)
            accs = [jnp.dot(a_ref[...], b[:, cs], preferred_element_type=F32) for b in b_refs]
            epilogue(accs, e_refs, o_refs, pl.program_id(0), s_refs, cs)

    return pl.pallas_call(
        kernel, grid=(nj, m // tm), in_specs=in_specs, out_specs=out_specs, out_shape=out_shape,
        scratch_shapes=list(scratch), compiler_params=_params(("parallel", "arbitrary")), name=name,
    )(a, *bs, *[e[0] for e in extras])


def _rows_out(m, n, dtype, tm, tn):
    return ((m, n), dtype, (tm, tn))


def _rope_lanes(x, c, sl, sr, half):
    return x * c + pltpu.roll(x, LANES - half, 1) * sl + pltpu.roll(x, half, 1) * sr


def _row_tables(tabs, tm):
    return [(t, (tm, LANES), lambda i, j: (i, 0)) for t in tabs]


def _proj_qkv_group(x_bf, w_qkv, tabs_a, g, dil, cfg):
    m, gw = cfg.m, cfg.gw
    tm = _tile(m, 1024)
    half = A_ROT_DIM // 2
    ngroups = len(DIL_GROUPS)

    def epilogue(accs, e, o, j, scr, cs):
        acc = accs[0]
        h0 = cs.start // LANES
        local = [slice(h * LANES, (h + 1) * LANES) for h in range((cs.stop - cs.start) // LANES)]

        def emit(piece):
            if dil == 1:
                for h, hs in enumerate(local):
                    lo = (h0 + h) * LANES
                    o[0][:, lo:lo + LANES] = piece(hs).astype(BF16)
                return
            for h, hs in enumerate(local):
                scr[0][h0 + h] = piece(hs)
            for r in range(dil):
                for h in range(len(local)):
                    lo = r * gw + (h0 + h) * LANES
                    o[0][:, lo:lo + LANES] = scr[0][h0 + h, pl.ds(r, tm // dil, stride=dil), :].astype(BF16)

        rot = j < 2
        c = jnp.where(rot, e[0][...], 1.0)
        sl = jnp.where(rot, e[1][...], 0.0)
        sr = jnp.where(rot, e[2][...], 0.0)
        emit(lambda hs: _rope_lanes(acc[:, hs], c, sl, sr, half))

    out_def = ((m // dil, 3 * dil * gw), BF16, (tm // dil, dil * gw))
    scratch = [pltpu.VMEM((gw // LANES, tm, LANES), F32)] if dil > 1 else []
    return _matmul(x_bf, [w_qkv], _row_tables(tabs_a, tm), [out_def], epilogue, tm=tm, tn=gw,
                   n_col_blocks=3, b_block=lambda j: j * ngroups + g, scratch=scratch,
                   col_split=max(1, gw // COL_SLAB), name=f"proj_qkv_g{g}")[0]


def _proj_latent(x_bf, w_lat, q_norm, kv_norm, tabs_b, cfg):
    m, ql, kl = cfg.m, cfg.q_lora, cfg.kv_lora
    n = ql + kl + LANES
    tm = _tile(m, 512)
    half = ROPE_DIM // 2

    def rms(v, g):
        return v * lax.rsqrt(jnp.mean(v * v, axis=-1, keepdims=True) + RMS_EPS) * g

    def epilogue(accs, e, o, j, scr, cs):
        acc = accs[0]
        o[0][...] = rms(acc[:, :ql], e[0][...]).astype(BF16)
        o[1][...] = rms(acc[:, ql:ql + kl], e[1][...]).astype(BF16)
        o[2][...] = _rope_lanes(acc[:, ql + kl:], e[2][...], e[3][...], e[4][...], half).astype(BF16)

    extras = [(q_norm.reshape(1, ql), (1, ql), lambda i, j: (0, 0)),
              (kv_norm.reshape(1, kl), (1, kl), lambda i, j: (0, 0))] + _row_tables(tabs_b, tm)
    outs = [_rows_out(m, ql, BF16, tm, ql), _rows_out(m, kl, BF16, tm, kl),
            _rows_out(m, LANES, BF16, tm, LANES)]
    return _matmul(x_bf, [w_lat], extras, outs, epilogue, tm=tm, tn=n, name="proj_latent")


def _proj_gates(x_bf, w_g, cfg):
    tm, tn = _tile(cfg.m, 1024), _tile(2 * cfg.d_model, 1024)

    def epilogue(accs, e, o, j, scr, cs):
        o[0][:, cs] = jax.nn.sigmoid(accs[0]).astype(BF16)

    return _matmul(x_bf, [w_g], [], [_rows_out(cfg.m, 2 * cfg.d_model, BF16, tm, tn)], epilogue,
                   tm=tm, tn=tn, col_split=tn // COL_SLAB, name="proj_gates")[0]


def _proj_qt(c_q, w_uq_t, cos_t, sin_t, cfg):
    m, ql = c_q.shape
    n = w_uq_t.shape[0]
    heads_per_tile = min(4, cfg.b_heads)
    tn = heads_per_tile * QK_DIM
    tm = _tile(m, 1024)
    half = ROPE_DIM // 2
    scale = QK_DIM ** -0.5 * LOG2_E

    def kernel(w_ref, c_ref, cos_ref, sin_ref, o_ref):
        acc = lax.dot_general(w_ref[...], c_ref[...], (((1,), (1,)), ((), ())),
                              preferred_element_type=F32) * scale
        cos, sin = cos_ref[...], sin_ref[...]
        for h in range(heads_per_tile):
            lo = h * QK_DIM
            o_ref[lo:lo + NOPE_DIM, :] = acc[lo:lo + NOPE_DIM, :].astype(BF16)
            x1 = acc[lo + NOPE_DIM:lo + NOPE_DIM + half, :]
            x2 = acc[lo + NOPE_DIM + half:lo + QK_DIM, :]
            o_ref[lo + NOPE_DIM:lo + NOPE_DIM + half, :] = (x1 * cos - x2 * sin).astype(BF16)
            o_ref[lo + NOPE_DIM + half:lo + QK_DIM, :] = (x1 * sin + x2 * cos).astype(BF16)

    tab = pl.BlockSpec((half, tm), lambda j, i: (0, i))
    return pl.pallas_call(
        kernel, grid=(n // tn, m // tm),
        in_specs=[pl.BlockSpec((tn, ql), lambda j, i: (j, 0)), pl.BlockSpec((tm, ql), lambda j, i: (i, 0)),
                  tab, tab],
        out_specs=pl.BlockSpec((tn, tm), lambda j, i: (j, i)),
        out_shape=jax.ShapeDtypeStruct((n, m), BF16),
        compiler_params=_params(("parallel", "arbitrary")), name="proj_qt",
    )(w_uq_t, c_q, cos_t, sin_t)


def _proj_kv(c_kv, w_ukv, cfg):
    m = cfg.m
    n = cfg.b_heads * (NOPE_DIM + V_DIM)
    tm, tn = _tile(m, 1024), _tile(n, 1024)
    slabs = tn // LANES

    def epilogue(accs, e, o, j, scr, cs):
        for s in range(slabs):
            o[0][s] = accs[0][:, s * LANES:(s + 1) * LANES].astype(BF16)

    out_def = ((n // LANES, m, LANES), BF16, (slabs, tm, LANES), lambda i, j: (j, i, 0))
    return _matmul(c_kv, [w_ukv], [], [out_def], epilogue, tm=tm, tn=tn, name="proj_kv")[0]


def _branch_a(out_a, w_ba, gates, cfg):
    tm, tn = _tile(cfg.m, 1024), _tile(cfg.d_model, 512)

    def epilogue(accs, e, o, j, scr, cs):
        o[0][:, cs] = e[0][:, cs].astype(F32) * accs[0]

    extras = [(gates, (tm, tn), lambda i, j: (i, j))]
    return _matmul(out_a, [w_ba], extras, [_rows_out(cfg.m, cfg.d_model, F32, tm, tn)], epilogue,
                   tm=tm, tn=tn, col_split=tn // COL_SLAB, name="branch_a")[0]


def _branch_b_merge(out_b, w_bb, gates, part_a, cfg):
    tm, tn = _tile(cfg.m, 512), _tile(cfg.d_model, 512)
    goff = cfg.d_model // tn

    def epilogue(accs, e, o, j, scr, cs):
        o[0][:, cs] = (e[1][:, cs] + e[0][:, cs].astype(F32) * accs[0]).astype(BF16)

    extras = [(gates, (tm, tn), lambda i, j: (i, j + goff)),
              (part_a, (tm, tn), lambda i, j: (i, j))]
    return _matmul(out_b, [w_bb], extras, [_rows_out(cfg.m, cfg.d_model, BF16, tm, tn)], epilogue,
                   tm=tm, tn=tn, col_split=tn // COL_SLAB, name="branch_b_merge")[0]


def _residual_matmul(a, w, x_res, cfg, *, tm, name):
    tm, tn = _tile(cfg.m, tm), _tile(cfg.d_model, 512)
    alpha = cfg.alpha

    def epilogue(accs, e, o, j, scr, cs):
        o[0][:, cs] = alpha * e[0][:, cs] + accs[0]

    extras = [(x_res, (tm, tn), lambda i, j: (i, j))]
    return _matmul(a, [w], extras, [_rows_out(cfg.m, cfg.d_model, F32, tm, tn)], epilogue,
                   tm=tm, tn=tn, col_split=tn // COL_SLAB, name=name)[0]


def _swiglu(x_bf, w_gate, w_up, cfg):
    tm, tn = _tile(cfg.m, 1024), _tile(cfg.ff_pad, 512)

    def epilogue(accs, e, o, j, scr, cs):
        g, u = accs
        o[0][:, cs] = (g * jax.nn.sigmoid(g) * u).astype(BF16)

    return _matmul(x_bf, [w_gate, w_up], [], [_rows_out(cfg.m, cfg.ff_pad, BF16, tm, tn)], epilogue,
                   tm=tm, tn=tn, col_split=tn // COL_SLAB, name="swiglu")[0]


def _layer_norm(y, g, b, cfg):
    m, d = y.shape
    tm = _tile(m, 256)

    def kernel(y_ref, g_ref, b_ref, o_ref, obf_ref):
        v = y_ref[...]
        mu = jnp.mean(v, axis=-1, keepdims=True)
        vc = v - mu
        var = jnp.mean(vc * vc, axis=-1, keepdims=True)
        out = vc * lax.rsqrt(var + LN_EPS) * g_ref[...] + b_ref[...]
        o_ref[...] = out
        obf_ref[...] = out.astype(BF16)

    row = pl.BlockSpec((tm, d), lambda i: (i, 0))
    vec = pl.BlockSpec((1, d), lambda i: (0, 0))
    return pl.pallas_call(
        kernel, grid=(m // tm,), in_specs=[row, vec, vec], out_specs=[row, row],
        out_shape=[jax.ShapeDtypeStruct((m, d), F32), jax.ShapeDtypeStruct((m, d), BF16)],
        compiler_params=_params(("parallel",)), name="layer_norm",
    )(y, g.reshape(1, d), b.reshape(1, d))


def _final_layer_norm(y, g, b, cfg):
    m, d = y.shape
    tm = _tile(cfg.sample_len, 256)
    p_blocks = cfg.prompt_len // tm

    def kernel(y_ref, g_ref, b_ref, op_ref, os_ref):
        v = y_ref[...]
        mu = jnp.mean(v, axis=-1, keepdims=True)
        vc = v - mu
        var = jnp.mean(vc * vc, axis=-1, keepdims=True)
        out = vc * lax.rsqrt(var + LN_EPS) * g_ref[...] + b_ref[...]
        i = pl.program_id(0)

        @pl.when(i < p_blocks)
        def _():
            op_ref[...] = out

        @pl.when(i >= p_blocks)
        def _():
            os_ref[...] = out

    row = pl.BlockSpec((tm, d), lambda i: (i, 0))
    vec = pl.BlockSpec((1, d), lambda i: (0, 0))
    p_spec = pl.BlockSpec((tm, d), lambda i: (jnp.minimum(i, p_blocks - 1), 0))
    s_spec = pl.BlockSpec((tm, d), lambda i: (jnp.maximum(i - p_blocks, 0), 0))
    return pl.pallas_call(
        kernel, grid=(m // tm,), in_specs=[row, vec, vec], out_specs=[p_spec, s_spec],
        out_shape=[jax.ShapeDtypeStruct((cfg.prompt_len, d), F32),
                   jax.ShapeDtypeStruct((m - cfg.prompt_len, d), F32)],
        compiler_params=_params(("arbitrary",)), name="final_layer_norm",
    )(y, g.reshape(1, d), b.reshape(1, d))


def _segment(row0, p_len, s_len):
    in_prompt = row0 < p_len
    s_idx = jnp.maximum(row0 - p_len, 0) // s_len
    lo = jnp.where(in_prompt, 0, p_len + s_idx * s_len)
    return lo, jnp.where(in_prompt, p_len, s_len)


def _banded_attention(qkv_g, g, dil, cfg):
    hg, gw = cfg.hg, cfg.gw
    rows = cfg.m // dil
    p_len, s_len = cfg.prompt_len // dil, cfg.sample_len // dil
    bq = 2 * BAND_W
    kw = bq + 2 * BAND_W
    rc = _tile(rows, 2048)
    nsub = rc // bq
    scale = A_HEAD_DIM ** -0.5

    def kernel(q_ref, k_ref, v_ref, o_ref, lse_ref):
        base = pl.program_id(2) * rc

        def body(t, carry):
            s0 = pl.multiple_of(t * bq, bq)
            g0 = base + s0
            ks = pl.multiple_of(jnp.clip(g0 - BAND_W, 0, rows - kw), BAND_W)
            q = q_ref[pl.ds(s0, bq), :]
            k = k_ref[pl.ds(ks, kw), :]
            v = v_ref[pl.ds(ks, kw), :]
            s = lax.dot_general(q, k, (((1,), (1,)), ((), ())), preferred_element_type=F32) * scale
            qrow = g0 + lax.broadcasted_iota(jnp.int32, (bq, kw), 0)
            krow = ks + lax.broadcasted_iota(jnp.int32, (bq, kw), 1)
            lo, n = _segment(g0, p_len, s_len)
            valid = (jnp.abs(krow - qrow) <= BAND_W) & (krow >= lo) & (krow < lo + n)
            s = jnp.where(valid, s, NEG_BIG)
            mx = jnp.max(s, axis=-1, keepdims=True)
            p = jnp.exp(s - mx)
            l = jnp.sum(p, axis=-1, keepdims=True)
            pv = jnp.dot(p.astype(BF16), v, preferred_element_type=F32)
            o_ref[pl.ds(s0, bq), :] = pv / l
            lse_ref[pl.ds(s0, bq), :] = jnp.broadcast_to(mx + jnp.log(l), (bq, LANES))
            return carry

        lax.fori_loop(0, nsub, body, 0, unroll=BAND_UNROLL)

    def col(kind):
        return lambda r, j, c: (0, (kind * dil + r) * hg + j)

    q_spec = pl.BlockSpec((rc, LANES), lambda r, j, c: (c, r * hg + j))
    k_spec = pl.BlockSpec((rows, LANES), col(1))
    v_spec = pl.BlockSpec((rows, LANES), col(2))
    o_spec = pl.BlockSpec((rc, LANES), lambda r, j, c: (c, r * hg + j))
    shp = jax.ShapeDtypeStruct((rows, dil * gw), F32)
    return pl.pallas_call(
        kernel, grid=(dil, hg, rows // rc), in_specs=[q_spec, k_spec, v_spec],
        out_specs=[o_spec, o_spec], out_shape=[shp, shp],
        compiler_params=_params(("parallel", "parallel", "arbitrary")), name=f"banded_attention_g{g}",
    )(qkv_g, qkv_g, qkv_g)


def _mix_groups(outs, lses, cfg):
    m, gw = cfg.m, cfg.gw
    tm = _tile(m, 256)
    ng = len(outs)
    dils = [d for _, d in DIL_GROUPS]

    def kernel(*refs):
        o_refs, l_refs, out_ref = refs[:ng], refs[ng:2 * ng], refs[2 * ng]
        scr = list(refs[2 * ng + 1:])

        def token_major(ref, d):
            if d == 1:
                return ref[...]
            buf = scr.pop()
            for r in range(d):
                for h in range(cfg.hg):
                    lo = r * gw + h * LANES
                    buf[h, pl.ds(r, tm // d, stride=d), :] = ref[:, lo:lo + LANES]
            return jnp.concatenate([buf[h] for h in range(cfg.hg)], axis=1)

        os_ = [token_major(r, d) for r, d in zip(o_refs, dils)]
        ls = [token_major(r, d) for r, d in zip(l_refs, dils)]
        mx = functools.reduce(jnp.maximum, ls)
        es = [jnp.exp(l - mx) for l in ls]
        inv = 1.0 / functools.reduce(jnp.add, es)
        for gi in range(ng):
            out_ref[:, gi * gw:(gi + 1) * gw] = (os_[gi] * (es[gi] * inv)).astype(BF16)

    specs = [pl.BlockSpec((tm // d, d * gw), lambda i: (i, 0)) for d in dils]
    n_scr = 2 * sum(d > 1 for d in dils)
    return pl.pallas_call(
        kernel, grid=(m // tm,), in_specs=specs * 2,
        out_specs=pl.BlockSpec((tm, ng * gw), lambda i: (i, 0)),
        out_shape=jax.ShapeDtypeStruct((m, ng * gw), BF16),
        scratch_shapes=[pltpu.VMEM((cfg.hg, tm, LANES), F32)] * n_scr,
        compiler_params=_params(("parallel",)), name="mix_groups",
    )(*outs, *lses)


def _latent_attention(qt, kv, kpe, cfg):
    m, h = cfg.m, cfg.b_heads
    p_len, s_len = cfg.prompt_len, cfg.sample_len
    tq = _tile(s_len, 512)
    tk = _tile(s_len // 4, 512)
    assert s_len % (4 * tk) == 0 and p_len % (4 * tk) == 0, "each sequence needs >= 2 chunk pairs"

    nq = 4
    assert s_len % (nq * tq) == 0 and p_len % (nq * tq) == 0
    assert m % tk == 0

    def kernel(q_ref, kn_ref, kpe_ref, v_ref, o_ref, vt_ref, *scratch_refs):
        @pl.when(pl.program_id(1) == 0)
        def _():
            def transpose_chunk(c, carry):
                r0 = pl.multiple_of(c * tk, tk)
                vt_ref[:, pl.ds(r0, tk)] = v_ref[pl.ds(r0, tk), :].T
                return carry
            lax.fori_loop(0, m // tk, transpose_chunk, 0)

        row0 = pl.program_id(1) * (nq * tq)
        lo, n = _segment(row0, p_len, s_len)
        npair = n // (2 * tk)
        streams = [scratch_refs[5 * u:5 * u + 5] for u in range(nq)]
        qts = [q_ref[:, u * tq:(u + 1) * tq] for u in range(nq)]

        def scores(u, c, s_ref):
            ks = pl.multiple_of(lo + c * tk, tk)
            k = jnp.concatenate([kn_ref[pl.ds(ks, tk), :], kpe_ref[pl.ds(ks, tk), :][:, :ROPE_DIM]],
                                axis=1)
            s = jnp.dot(k, qts[u], preferred_element_type=F32)
            s_ref[...] = s
            return jnp.max(s, axis=0, keepdims=True)

        def softmax(s_ref, p_ref, mx, chunk_max):
            mx_new = jnp.maximum(mx, chunk_max)
            p_ref[...] = jnp.exp2((s_ref[...] - mx_new).astype(BF16))
            return mx_new, jnp.exp2(mx - mx_new)

        ones_rows = jnp.ones((SUM_ROWS, tk), BF16)

        def values(c, p_ref, acc_ref, corr):
            ks = pl.multiple_of(lo + c * tk, tk)
            lhs = jnp.concatenate([vt_ref[:, pl.ds(ks, tk)], ones_rows], axis=0)
            acc_ref[...] = corr * acc_ref[...] + jnp.dot(lhs, p_ref[...], preferred_element_type=F32)

        def pair(i, carry, first=False, last=False):
            c = 2 * i
            us = range(nq)
            s_a, s_b, p_a, p_b, acc = zip(*streams)
            mx, max_a, corr_prev = (list(t) for t in zip(*carry))
            corr_a, corr_b = [None] * nq, [None] * nq
            max_b = [scores(u, c + 1, s_b[u]) for u in us]
            for u in us:
                mx[u], corr_a[u] = softmax(s_a[u], p_a[u], mx[u], max_a[u])
                if not first:
                    values(c - 1, p_b[u], acc[u], corr_prev[u])
            if not last:
                max_a = [scores(u, c + 2, s_a[u]) for u in us]
            for u in us:
                values(c, p_a[u], acc[u], corr_a[u])
                mx[u], corr_b[u] = softmax(s_b[u], p_b[u], mx[u], max_b[u])
            return tuple(zip(mx, max_a, corr_b))

        row = jnp.zeros((1, tq), F32)
        init = []
        for u, (s_a, _, _, _, acc_ref) in enumerate(streams):
            acc_ref[...] = jnp.zeros_like(acc_ref)
            init.append((row - jnp.inf, scores(u, 0, s_a), row))
        carry = pair(0, tuple(init), first=True)
        carry = lax.fori_loop(1, npair - 1, pair, carry)
        carry = pair(npair - 1, carry, last=True)
        for u, (_, _, _, p_b, acc_ref) in enumerate(streams):
            values(2 * npair - 1, p_b, acc_ref, carry[u][2])
            acc = acc_ref[...]
            o_ref[u * tq:(u + 1) * tq, :] = (acc[:V_DIM] / acc[V_DIM:V_DIM + 1]).T.astype(BF16)

    scratch = [pltpu.VMEM((V_DIM, m), BF16)]
    scratch += [pltpu.VMEM((tk, tq), F32), pltpu.VMEM((tk, tq), F32), pltpu.VMEM((tk, tq), BF16),
                pltpu.VMEM((tk, tq), BF16), pltpu.VMEM((V_DIM + SUM_ROWS, tq), F32)] * nq
    q_spec = pl.BlockSpec((QK_DIM, nq * tq), lambda hh, i: (hh, i))
    kn_spec = pl.BlockSpec((None, m, LANES), lambda hh, i: (2 * hh, 0, 0))
    kpe_spec = pl.BlockSpec((m, LANES), lambda hh, i: (0, 0))
    v_spec = pl.BlockSpec((None, m, LANES), lambda hh, i: (2 * hh + 1, 0, 0))
    o_spec = pl.BlockSpec((nq * tq, V_DIM), lambda hh, i: (i, hh))
    return pl.pallas_call(
        kernel, grid=(h, m // (nq * tq)), in_specs=[q_spec, kn_spec, kpe_spec, v_spec], out_specs=o_spec,
        out_shape=jax.ShapeDtypeStruct((m, h * V_DIM), BF16), scratch_shapes=scratch,
        compiler_params=_params(("parallel", "arbitrary")), name="latent_attention",
    )(qt, kv, kpe, kv)


def _rope_angles(pos, rot_dim):
    half = rot_dim // 2
    inv_freq = ROPE_THETA ** (-jnp.arange(half, dtype=F32) / half)
    return pos[:, None] * inv_freq[None, :]


def _rope_tables(pos, rot_dim):
    ang = _rope_angles(pos, rot_dim)
    cos, sin = jnp.cos(ang), jnp.sin(ang)
    rest = LANES - rot_dim
    ones = jnp.ones((pos.shape[0], rest), F32)
    zeros = jnp.zeros((pos.shape[0], rest), F32)
    zh = jnp.zeros_like(sin)
    c = jnp.concatenate([cos, cos, ones], axis=1)
    sl = jnp.concatenate([-sin, zh, zeros], axis=1)
    sr = jnp.concatenate([zh, sin, zeros], axis=1)
    return c, sl, sr


def _prep_weights(cfg, w_in, w_uq, w_ukv, w_ba, w_bb, w_o, w_gate, w_up, w_down):
    aw, ql, kl = cfg.a_width, cfg.q_lora, cfg.kv_lora
    off_cq = 3 * aw
    off_ga = off_cq + ql + kl + ROPE_DIM
    fpad = cfg.ff_pad - cfg.ff
    return dict(
        w_qkv=w_in[:, :off_cq].astype(BF16),
        w_lat=jnp.pad(w_in[:, off_cq:off_ga], ((0, 0), (0, LANES - ROPE_DIM))).astype(BF16),
        w_g=w_in[:, off_ga:].astype(BF16),
        w_uq_t=w_uq.T.astype(BF16), w_ukv=w_ukv.astype(BF16),
        w_ba=w_ba.astype(BF16), w_bb=w_bb.astype(BF16), w_o=w_o.astype(BF16),
        w_gate=jnp.pad(w_gate, ((0, 0), (0, fpad))).astype(BF16),
        w_up=jnp.pad(w_up, ((0, 0), (0, fpad))).astype(BF16),
        w_down=jnp.pad(w_down, ((0, fpad), (0, 0))).astype(BF16),
    )


def _layer(x, x_bf, w, q_norm, kv_norm, ln1_g, ln1_b, ln2_g, ln2_b, tabs_a, tabs_b, tabs_bt, cfg, last):
    c_q, c_kv, kpe = _proj_latent(x_bf, w["w_lat"], q_norm, kv_norm, tabs_b, cfg)
    gates = _proj_gates(x_bf, w["w_g"], cfg)

    outs, lses = [], []
    for g, (_, dil) in enumerate(DIL_GROUPS):
        qkv_g = _proj_qkv_group(x_bf, w["w_qkv"], tabs_a, g, dil, cfg)
        o, s = _banded_attention(qkv_g, g, dil, cfg)
        outs.append(o)
        lses.append(s)
    out_a = _mix_groups(outs, lses, cfg)

    qt = _proj_qt(c_q, w["w_uq_t"], *tabs_bt, cfg)
    kv = _proj_kv(c_kv, w["w_ukv"], cfg)
    out_b = _latent_attention(qt, kv, kpe, cfg)

    part_a = _branch_a(out_a, w["w_ba"], gates, cfg)
    merged = _branch_b_merge(out_b, w["w_bb"], gates, part_a, cfg)
    y1 = _residual_matmul(merged, w["w_o"], x, cfg, tm=1024, name="out_proj")
    x1, x1_bf = _layer_norm(y1, ln1_g, ln1_b, cfg)

    hmid = _swiglu(x1_bf, w["w_gate"], w["w_up"], cfg)
    y2 = _residual_matmul(hmid, w["w_down"], x1, cfg, tm=256, name="ffn_down")
    return (_final_layer_norm if last else _layer_norm)(y2, ln2_g, ln2_b, cfg)


def _trunk(cfg, x_prompt, x_sample, w_in, mla_q_norm, w_uq, mla_kv_norm, w_ukv, w_branch_a, w_branch_b,
           w_out, ln1_g, ln1_b, w_ffn_gate, w_ffn_up, w_ffn_down, ln2_g, ln2_b):
    d = cfg.d_model
    x = jnp.concatenate([x_prompt.reshape(-1, d), x_sample.reshape(-1, d)], axis=0)
    x_bf = x.astype(BF16)
    pos = jnp.concatenate([jnp.arange(cfg.prompt_len, dtype=F32),
                           jnp.tile(jnp.arange(cfg.sample_len, dtype=F32), cfg.n_sample)])
    tabs_a = _rope_tables(pos, A_ROT_DIM)
    tabs_b = _rope_tables(pos, ROPE_DIM)
    ang_t = _rope_angles(pos, ROPE_DIM).T
    tabs_bt = (jnp.cos(ang_t), jnp.sin(ang_t))
    for l in range(cfg.depth):
        w = _prep_weights(cfg, w_in[l], w_uq[l], w_ukv[l], w_branch_a[l], w_branch_b[l], w_out[l],
                          w_ffn_gate[l], w_ffn_up[l], w_ffn_down[l])
        last = l == cfg.depth - 1
        out = _layer(x, x_bf, w, mla_q_norm[l], mla_kv_norm[l], ln1_g[l], ln1_b[l], ln2_g[l],
                     ln2_b[l], tabs_a, tabs_b, tabs_bt, cfg, last)
        if not last:
            x, x_bf = out
    y_prompt, y_sample = out
    return (y_prompt.reshape(x_prompt.shape), y_sample.reshape(x_sample.shape))


def kernel(x_prompt, x_sample, w_in, mla_q_norm, w_uq, mla_kv_norm, w_ukv, w_branch_a, w_branch_b, w_out,
           ln1_g, ln1_b, w_ffn_gate, w_ffn_up, w_ffn_down, ln2_g, ln2_b):
    assert x_prompt.shape[0] == 1, "one prompt sequence"
    cfg = Cfg(d_model=x_prompt.shape[-1], prompt_len=x_prompt.shape[1], sample_len=x_sample.shape[1],
              n_sample=x_sample.shape[0], depth=w_in.shape[0],
              hg=w_branch_a.shape[1] // (len(DIL_GROUPS) * A_HEAD_DIM),
              b_heads=w_branch_b.shape[1] // V_DIM, q_lora=w_uq.shape[1], kv_lora=w_ukv.shape[1],
              ff=w_ffn_gate.shape[2])
    return _trunk(cfg, x_prompt, x_sample, w_in, mla_q_norm, w_uq, mla_kv_norm, w_ukv, w_branch_a,
                  w_branch_b, w_out, ln1_g, ln1_b, w_ffn_gate, w_ffn_up, w_ffn_down, ln2_g, ln2_b)
```

```python
import functools
from typing import NamedTuple

import jax
import jax.numpy as jnp
from jax import lax
from jax.experimental import pallas as pl
from jax.experimental.pallas import tpu as pltpu

F32 = jnp.float32
BF16 = jnp.bfloat16

LANES = 128
BF16_ROWS = 16
ROPE_THETA = 500000.0
LN_EPS = 1e-5
RMS_EPS = 1e-6
NEG_BIG = -1e30
LOG2_E = 1.4426950408889634

DIL_GROUPS = ((128, 1), (512, 4), (2048, 16))
BAND_W = 64
BAND_UNROLL = 8
A_HEAD_DIM = 128
A_ROT_DIM = 32
NOPE_DIM = 128
ROPE_DIM = 64
V_DIM = 128
QK_DIM = NOPE_DIM + ROPE_DIM
SUM_ROWS = BF16_ROWS
COL_SLAB = 256
VMEM_LIMIT_MB = 56


class Cfg(NamedTuple):
    d_model: int
    prompt_len: int
    sample_len: int
    n_sample: int
    depth: int
    hg: int
    b_heads: int
    q_lora: int
    kv_lora: int
    ff: int

    @property
    def m(self):
        return self.prompt_len + self.n_sample * self.sample_len

    @property
    def gw(self):
        return self.hg * A_HEAD_DIM

    @property
    def a_width(self):
        return len(DIL_GROUPS) * self.gw

    @property
    def ff_pad(self):
        return -(-self.ff // 1024) * 1024 if self.ff > 1024 else -(-self.ff // LANES) * LANES

    @property
    def alpha(self):
        return (2.0 * self.depth) ** 0.25


class LayerWeight(NamedTuple):
    stack: jax.Array
    layer: int


def _tile(n, pref):
    if n <= pref:
        return n
    t = (pref // LANES) * LANES
    while n % t:
        t -= LANES
    return t


def _params(sem):
    return pltpu.CompilerParams(dimension_semantics=sem,
                                vmem_limit_bytes=VMEM_LIMIT_MB * 1024 * 1024)


def _matmul(a, bs, extras, out_defs, epilogue, *, tm, tn, name, n_col_blocks=None, b_block=None,
            scratch=(), col_split=1):
    m, k_dim = a.shape
    nj = bs[0].stack.shape[2] // tn if n_col_blocks is None else n_col_blocks
    b_block = (lambda j: j) if b_block is None else b_block
    nb, ne, no = len(bs), len(extras), len(out_defs)
    swap = lambda f: (lambda j, i: f(i, j))

    a_spec = pl.BlockSpec((tm, k_dim), lambda j, i: (i, 0))
    b_specs = [pl.BlockSpec((None, k_dim, tn), lambda j, i, layer=b.layer: (layer, 0, b_block(j)))
               for b in bs]
    in_specs = [a_spec] + b_specs + [pl.BlockSpec(bshape, swap(f)) for _, bshape, f in extras]
    out_specs = [pl.BlockSpec(d[2], swap(d[3] if len(d) > 3 else (lambda i, j: (i, j)))) for d in out_defs]
    out_shape = [jax.ShapeDtypeStruct(d[0], d[1]) for d in out_defs]

    def kernel(*refs):
        a_ref = refs[0]
        b_refs = refs[1:1 + nb]
        e_refs = refs[1 + nb:1 + nb + ne]
        o_refs = refs[1 + nb + ne:1 + nb + ne + no]
        s_refs = refs[1 + nb + ne + no:]
        width = tn // col_split
        for s in range(col_split):
            cs = slice(s * width, (s + 1) * width)
            accs = [jnp.dot(a_ref[...], b[:, cs], preferred_element_type=F32) for b in b_refs]
            epilogue(accs, e_refs, o_refs, pl.program_id(0), s_refs, cs)

    return pl.pallas_call(
        kernel, grid=(nj, m // tm), in_specs=in_specs, out_specs=out_specs, out_shape=out_shape,
        scratch_shapes=list(scratch), compiler_params=_params(("parallel", "arbitrary")), name=name,
    )(a, *[b.stack for b in bs], *[e[0] for e in extras])


def _rows_out(m, n, dtype, tm, tn):
    return ((m, n), dtype, (tm, tn))


def _rope_lanes(x, c, sl, sr, half):
    return x * c + pltpu.roll(x, LANES - half, 1) * sl + pltpu.roll(x, half, 1) * sr


def _row_tables(tabs, tm):
    return [(t, (tm, LANES), lambda i, j: (i, 0)) for t in tabs]


def _proj_qkv_group(x_bf, w_qkv, tabs_a, g, dil, cfg):
    m, gw = cfg.m, cfg.gw
    tm = _tile(m, 1024)
    half = A_ROT_DIM // 2
    ngroups = len(DIL_GROUPS)

    def epilogue(accs, e, o, j, scr, cs):
        acc = accs[0]
        h0 = cs.start // LANES
        local = [slice(h * LANES, (h + 1) * LANES) for h in range((cs.stop - cs.start) // LANES)]

        def emit(piece):
            if dil == 1:
                for h, hs in enumerate(local):
                    lo = (h0 + h) * LANES
                    o[0][:, lo:lo + LANES] = piece(hs).astype(BF16)
                return
            for h, hs in enumerate(local):
                scr[0][h0 + h] = piece(hs)
            for r in range(dil):
                for h in range(len(local)):
                    lo = r * gw + (h0 + h) * LANES
                    o[0][:, lo:lo + LANES] = scr[0][h0 + h, pl.ds(r, tm // dil, stride=dil), :].astype(BF16)

        rot = j < 2
        c = jnp.where(rot, e[0][...], 1.0)
        sl = jnp.where(rot, e[1][...], 0.0)
        sr = jnp.where(rot, e[2][...], 0.0)
        emit(lambda hs: _rope_lanes(acc[:, hs], c, sl, sr, half))

    out_def = ((m // dil, 3 * dil * gw), BF16, (tm // dil, dil * gw))
    scratch = [pltpu.VMEM((gw // LANES, tm, LANES), F32)] if dil > 1 else []
    return _matmul(x_bf, [w_qkv], _row_tables(tabs_a, tm), [out_def], epilogue, tm=tm, tn=gw,
                   n_col_blocks=3, b_block=lambda j: j * ngroups + g, scratch=scratch,
                   col_split=max(1, gw // COL_SLAB), name=f"proj_qkv_g{g}")[0]


def _proj_latent(x_bf, w_lat, q_norm, kv_norm, tabs_b, cfg):
    m, ql, kl = cfg.m, cfg.q_lora, cfg.kv_lora
    n = ql + kl + LANES
    tm = _tile(m, 512)
    half = ROPE_DIM // 2

    def rms(v, g):
        return v * lax.rsqrt(jnp.mean(v * v, axis=-1, keepdims=True) + RMS_EPS) * g

    def epilogue(accs, e, o, j, scr, cs):
        acc = accs[0]
        o[0][...] = rms(acc[:, :ql], e[0][...]).astype(BF16)
        o[1][...] = rms(acc[:, ql:ql + kl], e[1][...]).astype(BF16)
        o[2][...] = _rope_lanes(acc[:, ql + kl:], e[2][...], e[3][...], e[4][...], half).astype(BF16)

    extras = [(q_norm.reshape(1, ql), (1, ql), lambda i, j: (0, 0)),
              (kv_norm.reshape(1, kl), (1, kl), lambda i, j: (0, 0))] + _row_tables(tabs_b, tm)
    outs = [_rows_out(m, ql, BF16, tm, ql), _rows_out(m, kl, BF16, tm, kl),
            _rows_out(m, LANES, BF16, tm, LANES)]
    return _matmul(x_bf, [w_lat], extras, outs, epilogue, tm=tm, tn=n, name="proj_latent")


def _proj_gates(x_bf, w_g, cfg):
    tm, tn = _tile(cfg.m, 1024), _tile(2 * cfg.d_model, 1024)

    def epilogue(accs, e, o, j, scr, cs):
        o[0][:, cs] = jax.nn.sigmoid(accs[0]).astype(BF16)

    return _matmul(x_bf, [w_g], [], [_rows_out(cfg.m, 2 * cfg.d_model, BF16, tm, tn)], epilogue,
                   tm=tm, tn=tn, col_split=tn // COL_SLAB, name="proj_gates")[0]


def _proj_qt(c_q, w_uq_t, cos_t, sin_t, cfg):
    m, ql = c_q.shape
    n = w_uq_t.stack.shape[1]
    heads_per_tile = min(4, cfg.b_heads)
    tn = heads_per_tile * QK_DIM
    tm = _tile(m, 2048)
    half = ROPE_DIM // 2
    scale = QK_DIM ** -0.5 * LOG2_E

    def kernel(w_ref, c_ref, cos_ref, sin_ref, o_ref):
        acc = lax.dot_general(w_ref[...], c_ref[...], (((1,), (1,)), ((), ())),
                              preferred_element_type=F32) * scale
        cos, sin = cos_ref[...], sin_ref[...]
        for h in range(heads_per_tile):
            lo = h * QK_DIM
            o_ref[lo:lo + NOPE_DIM, :] = acc[lo:lo + NOPE_DIM, :].astype(BF16)
            x1 = acc[lo + NOPE_DIM:lo + NOPE_DIM + half, :]
            x2 = acc[lo + NOPE_DIM + half:lo + QK_DIM, :]
            o_ref[lo + NOPE_DIM:lo + NOPE_DIM + half, :] = (x1 * cos - x2 * sin).astype(BF16)
            o_ref[lo + NOPE_DIM + half:lo + QK_DIM, :] = (x1 * sin + x2 * cos).astype(BF16)

    tab = pl.BlockSpec((half, tm), lambda j, i: (0, i))
    return pl.pallas_call(
        kernel, grid=(n // tn, m // tm),
        in_specs=[pl.BlockSpec((None, tn, ql), lambda j, i: (w_uq_t.layer, j, 0)),
                  pl.BlockSpec((tm, ql), lambda j, i: (i, 0)), tab, tab],
        out_specs=pl.BlockSpec((tn, tm), lambda j, i: (j, i)),
        out_shape=jax.ShapeDtypeStruct((n, m), BF16),
        compiler_params=_params(("parallel", "arbitrary")), name="proj_qt",
    )(w_uq_t.stack, c_q, cos_t, sin_t)


def _proj_kv(c_kv, w_ukv, cfg):
    m = cfg.m
    n = cfg.b_heads * (NOPE_DIM + V_DIM)
    tm, tn = _tile(m, 2048), _tile(n, 1024)
    slabs = tn // LANES

    def epilogue(accs, e, o, j, scr, cs):
        for s in range(slabs):
            o[0][s] = accs[0][:, s * LANES:(s + 1) * LANES].astype(BF16)

    out_def = ((n // LANES, m, LANES), BF16, (slabs, tm, LANES), lambda i, j: (j, i, 0))
    return _matmul(c_kv, [w_ukv], [], [out_def], epilogue, tm=tm, tn=tn, name="proj_kv")[0]


def _branch_a(out_a, w_ba, gates, cfg):
    tm, tn = _tile(cfg.m, 1024), _tile(cfg.d_model, 512)

    def epilogue(accs, e, o, j, scr, cs):
        o[0][:, cs] = e[0][:, cs].astype(F32) * accs[0]

    extras = [(gates, (tm, tn), lambda i, j: (i, j))]
    return _matmul(out_a, [w_ba], extras, [_rows_out(cfg.m, cfg.d_model, F32, tm, tn)], epilogue,
                   tm=tm, tn=tn, col_split=tn // COL_SLAB, name="branch_a")[0]


def _branch_b_merge(out_b, w_bb, gates, part_a, cfg):
    tm, tn = _tile(cfg.m, 512), _tile(cfg.d_model, 512)
    goff = cfg.d_model // tn

    def epilogue(accs, e, o, j, scr, cs):
        o[0][:, cs] = (e[1][:, cs] + e[0][:, cs].astype(F32) * accs[0]).astype(BF16)

    extras = [(gates, (tm, tn), lambda i, j: (i, j + goff)),
              (part_a, (tm, tn), lambda i, j: (i, j))]
    return _matmul(out_b, [w_bb], extras, [_rows_out(cfg.m, cfg.d_model, BF16, tm, tn)], epilogue,
                   tm=tm, tn=tn, col_split=tn // COL_SLAB, name="branch_b_merge")[0]


def _residual_matmul(a, w, x_res, cfg, *, tm, name):
    tm, tn = _tile(cfg.m, tm), _tile(cfg.d_model, 512)
    alpha = cfg.alpha

    def epilogue(accs, e, o, j, scr, cs):
        o[0][:, cs] = alpha * e[0][:, cs] + accs[0]

    extras = [(x_res, (tm, tn), lambda i, j: (i, j))]
    return _matmul(a, [w], extras, [_rows_out(cfg.m, cfg.d_model, F32, tm, tn)], epilogue,
                   tm=tm, tn=tn, col_split=tn // COL_SLAB, name=name)[0]


def _swiglu(x_bf, w_gate, w_up, cfg):
    tm, tn = _tile(cfg.m, 1024), _tile(cfg.ff_pad, 512)

    def epilogue(accs, e, o, j, scr, cs):
        g, u = accs
        o[0][:, cs] = (g * jax.nn.sigmoid(g) * u).astype(BF16)

    return _matmul(x_bf, [w_gate, w_up], [], [_rows_out(cfg.m, cfg.ff_pad, BF16, tm, tn)], epilogue,
                   tm=tm, tn=tn, col_split=tn // COL_SLAB, name="swiglu")[0]


def _layer_norm(y, g, b, cfg):
    m, d = y.shape
    tm = _tile(m, 256)

    def kernel(y_ref, g_ref, b_ref, o_ref, obf_ref):
        v = y_ref[...]
        mu = jnp.mean(v, axis=-1, keepdims=True)
        vc = v - mu
        var = jnp.mean(vc * vc, axis=-1, keepdims=True)
        out = vc * lax.rsqrt(var + LN_EPS) * g_ref[...] + b_ref[...]
        o_ref[...] = out
        obf_ref[...] = out.astype(BF16)

    row = pl.BlockSpec((tm, d), lambda i: (i, 0))
    vec = pl.BlockSpec((1, d), lambda i: (0, 0))
    return pl.pallas_call(
        kernel, grid=(m // tm,), in_specs=[row, vec, vec], out_specs=[row, row],
        out_shape=[jax.ShapeDtypeStruct((m, d), F32), jax.ShapeDtypeStruct((m, d), BF16)],
        compiler_params=_params(("parallel",)), name="layer_norm",
    )(y, g.reshape(1, d), b.reshape(1, d))


def _final_layer_norm(y, g, b, cfg):
    m, d = y.shape
    tm = _tile(cfg.sample_len, 256)
    p_blocks = cfg.prompt_len // tm

    def kernel(y_ref, g_ref, b_ref, op_ref, os_ref):
        v = y_ref[...]
        mu = jnp.mean(v, axis=-1, keepdims=True)
        vc = v - mu
        var = jnp.mean(vc * vc, axis=-1, keepdims=True)
        out = vc * lax.rsqrt(var + LN_EPS) * g_ref[...] + b_ref[...]
        i = pl.program_id(0)

        @pl.when(i < p_blocks)
        def _():
            op_ref[...] = out

        @pl.when(i >= p_blocks)
        def _():
            os_ref[...] = out

    row = pl.BlockSpec((tm, d), lambda i: (i, 0))
    vec = pl.BlockSpec((1, d), lambda i: (0, 0))
    p_spec = pl.BlockSpec((tm, d), lambda i: (jnp.minimum(i, p_blocks - 1), 0))
    s_spec = pl.BlockSpec((tm, d), lambda i: (jnp.maximum(i - p_blocks, 0), 0))
    return pl.pallas_call(
        kernel, grid=(m // tm,), in_specs=[row, vec, vec], out_specs=[p_spec, s_spec],
        out_shape=[jax.ShapeDtypeStruct((cfg.prompt_len, d), F32),
                   jax.ShapeDtypeStruct((m - cfg.prompt_len, d), F32)],
        compiler_params=_params(("arbitrary",)), name="final_layer_norm",
    )(y, g.reshape(1, d), b.reshape(1, d))


def _segment(row0, p_len, s_len):
    in_prompt = row0 < p_len
    s_idx = jnp.maximum(row0 - p_len, 0) // s_len
    lo = jnp.where(in_prompt, 0, p_len + s_idx * s_len)
    return lo, jnp.where(in_prompt, p_len, s_len)


def _banded_attention(qkv_g, g, dil, cfg):
    hg, gw = cfg.hg, cfg.gw
    rows = cfg.m // dil
    p_len, s_len = cfg.prompt_len // dil, cfg.sample_len // dil
    bq = 2 * BAND_W
    kw = bq + 2 * BAND_W
    rc = _tile(rows, 2048)
    nsub = rc // bq
    scale = A_HEAD_DIM ** -0.5

    def kernel(q_ref, k_ref, v_ref, o_ref, lse_ref):
        base = pl.program_id(2) * rc

        def body(t, carry):
            s0 = pl.multiple_of(t * bq, bq)
            g0 = base + s0
            ks = pl.multiple_of(jnp.clip(g0 - BAND_W, 0, rows - kw), BAND_W)
            q = q_ref[pl.ds(s0, bq), :]
            k = k_ref[pl.ds(ks, kw), :]
            v = v_ref[pl.ds(ks, kw), :]
            s = lax.dot_general(q, k, (((1,), (1,)), ((), ())), preferred_element_type=F32) * scale
            qrow = g0 + lax.broadcasted_iota(jnp.int32, (bq, kw), 0)
            krow = ks + lax.broadcasted_iota(jnp.int32, (bq, kw), 1)
            lo, n = _segment(g0, p_len, s_len)
            valid = (jnp.abs(krow - qrow) <= BAND_W) & (krow >= lo) & (krow < lo + n)
            s = jnp.where(valid, s, NEG_BIG)
            mx = jnp.max(s, axis=-1, keepdims=True)
            p = jnp.exp(s - mx)
            l = jnp.sum(p, axis=-1, keepdims=True)
            pv = jnp.dot(p.astype(BF16), v, preferred_element_type=F32)
            o_ref[pl.ds(s0, bq), :] = pv / l
            lse_ref[pl.ds(s0, bq), :] = jnp.broadcast_to(mx + jnp.log(l), (bq, LANES))
            return carry

        lax.fori_loop(0, nsub, body, 0, unroll=BAND_UNROLL)

    def col(kind):
        return lambda r, j, c: (0, (kind * dil + r) * hg + j)

    q_spec = pl.BlockSpec((rc, LANES), lambda r, j, c: (c, r * hg + j))
    k_spec = pl.BlockSpec((rows, LANES), col(1))
    v_spec = pl.BlockSpec((rows, LANES), col(2))
    o_spec = pl.BlockSpec((rc, LANES), lambda r, j, c: (c, r * hg + j))
    shp = jax.ShapeDtypeStruct((rows, dil * gw), F32)
    return pl.pallas_call(
        kernel, grid=(dil, hg, rows // rc), in_specs=[q_spec, k_spec, v_spec],
        out_specs=[o_spec, o_spec], out_shape=[shp, shp],
        compiler_params=_params(("parallel", "parallel", "arbitrary")), name=f"banded_attention_g{g}",
    )(qkv_g, qkv_g, qkv_g)


def _mix_groups(outs, lses, cfg):
    m, gw = cfg.m, cfg.gw
    tm = _tile(m, 256)
    ng = len(outs)
    dils = [d for _, d in DIL_GROUPS]

    def kernel(*refs):
        o_refs, l_refs, out_ref = refs[:ng], refs[ng:2 * ng], refs[2 * ng]
        scr = list(refs[2 * ng + 1:])

        def token_major(ref, d):
            if d == 1:
                return ref[...]
            buf = scr.pop()
            for r in range(d):
                for h in range(cfg.hg):
                    lo = r * gw + h * LANES
                    buf[h, pl.ds(r, tm // d, stride=d), :] = ref[:, lo:lo + LANES]
            return jnp.concatenate([buf[h] for h in range(cfg.hg)], axis=1)

        os_ = [token_major(r, d) for r, d in zip(o_refs, dils)]
        ls = [token_major(r, d) for r, d in zip(l_refs, dils)]
        mx = functools.reduce(jnp.maximum, ls)
        es = [jnp.exp(l - mx) for l in ls]
        inv = 1.0 / functools.reduce(jnp.add, es)
        for gi in range(ng):
            out_ref[:, gi * gw:(gi + 1) * gw] = (os_[gi] * (es[gi] * inv)).astype(BF16)

    specs = [pl.BlockSpec((tm // d, d * gw), lambda i: (i, 0)) for d in dils]
    n_scr = 2 * sum(d > 1 for d in dils)
    return pl.pallas_call(
        kernel, grid=(m // tm,), in_specs=specs * 2,
        out_specs=pl.BlockSpec((tm, ng * gw), lambda i: (i, 0)),
        out_shape=jax.ShapeDtypeStruct((m, ng * gw), BF16),
        scratch_shapes=[pltpu.VMEM((cfg.hg, tm, LANES), F32)] * n_scr,
        compiler_params=_params(("parallel",)), name="mix_groups",
    )(*outs, *lses)


def _latent_attention(qt, kv, kpe, cfg):
    m, h = cfg.m, cfg.b_heads
    p_len, s_len = cfg.prompt_len, cfg.sample_len
    tq = _tile(s_len, 512)
    tk = _tile(s_len // 4, 512)
    assert s_len % (4 * tk) == 0 and p_len % (4 * tk) == 0, "each sequence needs >= 2 chunk pairs"

    nq = 4
    assert s_len % (nq * tq) == 0 and p_len % (nq * tq) == 0
    assert m % tk == 0

    def kernel(q_ref, kn_ref, kpe_ref, v_ref, o_ref, vt_ref, *scratch_refs):
        @pl.when(pl.program_id(1) == 0)
        def _():
            def transpose_chunk(c, carry):
                r0 = pl.multiple_of(c * tk, tk)
                vt_ref[:, pl.ds(r0, tk)] = v_ref[pl.ds(r0, tk), :].T
                return carry
            lax.fori_loop(0, m // tk, transpose_chunk, 0)

        row0 = pl.program_id(1) * (nq * tq)
        lo, n = _segment(row0, p_len, s_len)
        npair = n // (2 * tk)
        streams = [scratch_refs[5 * u:5 * u + 5] for u in range(nq)]
        qts = [q_ref[:, u * tq:(u + 1) * tq] for u in range(nq)]

        def scores(u, c, s_ref):
            ks = pl.multiple_of(lo + c * tk, tk)
            k = jnp.concatenate([kn_ref[pl.ds(ks, tk), :], kpe_ref[pl.ds(ks, tk), :][:, :ROPE_DIM]],
                                axis=1)
            s = jnp.dot(k, qts[u], preferred_element_type=F32)
            s_ref[...] = s
            return jnp.max(s, axis=0, keepdims=True)

        def softmax(s_ref, p_ref, mx, chunk_max):
            mx_new = jnp.maximum(mx, chunk_max)
            p_ref[...] = jnp.exp2((s_ref[...] - mx_new).astype(BF16))
            return mx_new, jnp.exp2(mx - mx_new)

        ones_rows = jnp.ones((SUM_ROWS, tk), BF16)

        def values(c, p_ref, acc_ref, corr):
            ks = pl.multiple_of(lo + c * tk, tk)
            lhs = jnp.concatenate([vt_ref[:, pl.ds(ks, tk)], ones_rows], axis=0)
            acc_ref[...] = corr * acc_ref[...] + jnp.dot(lhs, p_ref[...], preferred_element_type=F32)

        def pair(i, carry, first=False, last=False):
            c = 2 * i
            us = range(nq)
            s_a, s_b, p_a, p_b, acc = zip(*streams)
            mx, max_a, corr_prev = (list(t) for t in zip(*carry))
            corr_a, corr_b = [None] * nq, [None] * nq
            max_b = [scores(u, c + 1, s_b[u]) for u in us]
            for u in us:
                mx[u], corr_a[u] = softmax(s_a[u], p_a[u], mx[u], max_a[u])
                if not first:
                    values(c - 1, p_b[u], acc[u], corr_prev[u])
            if not last:
                max_a = [scores(u, c + 2, s_a[u]) for u in us]
            for u in us:
                values(c, p_a[u], acc[u], corr_a[u])
                mx[u], corr_b[u] = softmax(s_b[u], p_b[u], mx[u], max_b[u])
            return tuple(zip(mx, max_a, corr_b))

        row = jnp.zeros((1, tq), F32)
        init = []
        for u, (s_a, _, _, _, acc_ref) in enumerate(streams):
            acc_ref[...] = jnp.zeros_like(acc_ref)
            init.append((row - jnp.inf, scores(u, 0, s_a), row))
        carry = pair(0, tuple(init), first=True)
        carry = lax.fori_loop(1, npair - 1, pair, carry)
        carry = pair(npair - 1, carry, last=True)
        for u, (_, _, _, p_b, acc_ref) in enumerate(streams):
            values(2 * npair - 1, p_b, acc_ref, carry[u][2])
            acc = acc_ref[...]
            o_ref[u * tq:(u + 1) * tq, :] = (acc[:V_DIM] / acc[V_DIM:V_DIM + 1]).T.astype(BF16)

    scratch = [pltpu.VMEM((V_DIM, m), BF16)]
    scratch += [pltpu.VMEM((tk, tq), F32), pltpu.VMEM((tk, tq), F32), pltpu.VMEM((tk, tq), BF16),
                pltpu.VMEM((tk, tq), BF16), pltpu.VMEM((V_DIM + SUM_ROWS, tq), F32)] * nq
    q_spec = pl.BlockSpec((QK_DIM, nq * tq), lambda hh, i: (hh, i))
    kn_spec = pl.BlockSpec((None, m, LANES), lambda hh, i: (2 * hh, 0, 0))
    kpe_spec = pl.BlockSpec((m, LANES), lambda hh, i: (0, 0))
    v_spec = pl.BlockSpec((None, m, LANES), lambda hh, i: (2 * hh + 1, 0, 0))
    o_spec = pl.BlockSpec((nq * tq, V_DIM), lambda hh, i: (i, hh))
    return pl.pallas_call(
        kernel, grid=(h, m // (nq * tq)), in_specs=[q_spec, kn_spec, kpe_spec, v_spec], out_specs=o_spec,
        out_shape=jax.ShapeDtypeStruct((m, h * V_DIM), BF16), scratch_shapes=scratch,
        compiler_params=_params(("parallel", "arbitrary")), name="latent_attention",
    )(qt, kv, kpe, kv)


def _rope_angles(pos, rot_dim):
    half = rot_dim // 2
    inv_freq = ROPE_THETA ** (-jnp.arange(half, dtype=F32) / half)
    return pos[:, None] * inv_freq[None, :]


def _rope_tables(pos, rot_dim):
    ang = _rope_angles(pos, rot_dim)
    cos, sin = jnp.cos(ang), jnp.sin(ang)
    rest = LANES - rot_dim
    ones = jnp.ones((pos.shape[0], rest), F32)
    zeros = jnp.zeros((pos.shape[0], rest), F32)
    zh = jnp.zeros_like(sin)
    c = jnp.concatenate([cos, cos, ones], axis=1)
    sl = jnp.concatenate([-sin, zh, zeros], axis=1)
    sr = jnp.concatenate([zh, sin, zeros], axis=1)
    return c, sl, sr


def _prep_weights(cfg, w_in, w_uq, w_ukv, w_ba, w_bb, w_o, w_gate, w_up, w_down):
    aw, ql, kl = cfg.a_width, cfg.q_lora, cfg.kv_lora
    off_cq = 3 * aw
    off_ga = off_cq + ql + kl + ROPE_DIM
    fpad = cfg.ff_pad - cfg.ff
    return dict(
        w_qkv=w_in[:, :, :off_cq].astype(BF16),
        w_lat=jnp.pad(w_in[:, :, off_cq:off_ga], ((0, 0), (0, 0), (0, LANES - ROPE_DIM))).astype(BF16),
        w_g=w_in[:, :, off_ga:].astype(BF16),
        w_uq_t=jnp.swapaxes(w_uq, 1, 2).astype(BF16), w_ukv=w_ukv.astype(BF16),
        w_ba=w_ba.astype(BF16), w_bb=w_bb.astype(BF16), w_o=w_o.astype(BF16),
        w_gate=jnp.pad(w_gate, ((0, 0), (0, 0), (0, fpad))).astype(BF16),
        w_up=jnp.pad(w_up, ((0, 0), (0, 0), (0, fpad))).astype(BF16),
        w_down=jnp.pad(w_down, ((0, 0), (0, fpad), (0, 0))).astype(BF16),
    )


def _layer(x, x_bf, w, q_norm, kv_norm, ln1_g, ln1_b, ln2_g, ln2_b, tabs_a, tabs_b, tabs_bt, cfg, last):
    c_q, c_kv, kpe = _proj_latent(x_bf, w["w_lat"], q_norm, kv_norm, tabs_b, cfg)
    gates = _proj_gates(x_bf, w["w_g"], cfg)

    outs, lses = [], []
    for g, (_, dil) in enumerate(DIL_GROUPS):
        qkv_g = _proj_qkv_group(x_bf, w["w_qkv"], tabs_a, g, dil, cfg)
        o, s = _banded_attention(qkv_g, g, dil, cfg)
        outs.append(o)
        lses.append(s)
    out_a = _mix_groups(outs, lses, cfg)

    qt = _proj_qt(c_q, w["w_uq_t"], *tabs_bt, cfg)
    kv = _proj_kv(c_kv, w["w_ukv"], cfg)
    out_b = _latent_attention(qt, kv, kpe, cfg)

    part_a = _branch_a(out_a, w["w_ba"], gates, cfg)
    merged = _branch_b_merge(out_b, w["w_bb"], gates, part_a, cfg)
    y1 = _residual_matmul(merged, w["w_o"], x, cfg, tm=1024, name="out_proj")
    x1, x1_bf = _layer_norm(y1, ln1_g, ln1_b, cfg)

    hmid = _swiglu(x1_bf, w["w_gate"], w["w_up"], cfg)
    y2 = _residual_matmul(hmid, w["w_down"], x1, cfg, tm=256, name="ffn_down")
    return (_final_layer_norm if last else _layer_norm)(y2, ln2_g, ln2_b, cfg)


def _trunk(cfg, x_prompt, x_sample, w_in, mla_q_norm, w_uq, mla_kv_norm, w_ukv, w_branch_a, w_branch_b,
           w_out, ln1_g, ln1_b, w_ffn_gate, w_ffn_up, w_ffn_down, ln2_g, ln2_b):
    d = cfg.d_model
    x = jnp.concatenate([x_prompt.reshape(-1, d), x_sample.reshape(-1, d)], axis=0)
    x_bf = x.astype(BF16)
    pos = jnp.concatenate([jnp.arange(cfg.prompt_len, dtype=F32),
                           jnp.tile(jnp.arange(cfg.sample_len, dtype=F32), cfg.n_sample)])
    tabs_a = _rope_tables(pos, A_ROT_DIM)
    tabs_b = _rope_tables(pos, ROPE_DIM)
    ang_t = _rope_angles(pos, ROPE_DIM).T
    tabs_bt = (jnp.cos(ang_t), jnp.sin(ang_t))
    stacks = _prep_weights(cfg, w_in, w_uq, w_ukv, w_branch_a, w_branch_b, w_out, w_ffn_gate, w_ffn_up,
                           w_ffn_down)
    for l in range(cfg.depth):
        w = {name: LayerWeight(stack, l) for name, stack in stacks.items()}
        last = l == cfg.depth - 1
        out = _layer(x, x_bf, w, mla_q_norm[l], mla_kv_norm[l], ln1_g[l], ln1_b[l], ln2_g[l],
                     ln2_b[l], tabs_a, tabs_b, tabs_bt, cfg, last)
        if not last:
            x, x_bf = out
    y_prompt, y_sample = out
    return (y_prompt.reshape(x_prompt.shape), y_sample.reshape(x_sample.shape))


def kernel(x_prompt, x_sample, w_in, mla_q_norm, w_uq, mla_kv_norm, w_ukv, w_branch_a, w_branch_b, w_out,
           ln1_g, ln1_b, w_ffn_gate, w_ffn_up, w_ffn_down, ln2_g, ln2_b):
    assert x_prompt.shape[0] == 1, "one prompt sequence"
    cfg = Cfg(d_model=x_prompt.shape[-1], prompt_len=x_prompt.shape[1], sample_len=x_sample.shape[1],
              n_sample=x_sample.shape[0], depth=w_in.shape[0],
              hg=w_branch_a.shape[1] // (len(DIL_GROUPS) * A_HEAD_DIM),
              b_heads=w_branch_b.shape[1] // V_DIM, q_lora=w_uq.shape[1], kv_lora=w_ukv.shape[1],
              ff=w_ffn_gate.shape[2])
    return _trunk(cfg, x_prompt, x_sample, w_in, mla_q_norm, w_uq, mla_kv_norm, w_ukv, w_branch_a,
                  w_branch_b, w_out, ln1_g, ln1_b, w_ffn_gate, w_ffn_up, w_ffn_down, ln2_g, ln2_b)
```

```python
import functools
from typing import NamedTuple

import jax
import jax.numpy as jnp
from jax import lax
from jax.experimental import pallas as pl
from jax.experimental.pallas import tpu as pltpu

F32 = jnp.float32
BF16 = jnp.bfloat16

LANES = 128
BF16_ROWS = 16
ROPE_THETA = 500000.0
LN_EPS = 1e-5
RMS_EPS = 1e-6
NEG_BIG = -1e30
LOG2_E = 1.4426950408889634

DIL_GROUPS = ((128, 1), (512, 4), (2048, 16))
BAND_W = 64
BAND_UNROLL = 8
A_HEAD_DIM = 128
A_ROT_DIM = 32
NOPE_DIM = 128
ROPE_DIM = 64
V_DIM = 128
QK_DIM = NOPE_DIM + ROPE_DIM
SUM_ROWS = BF16_ROWS
COL_SLAB = 256
VMEM_LIMIT_MB = 56


class Cfg(NamedTuple):
    d_model: int
    prompt_len: int
    sample_len: int
    n_sample: int
    depth: int
    hg: int
    b_heads: int
    q_lora: int
    kv_lora: int
    ff: int

    @property
    def m(self):
        return self.prompt_len + self.n_sample * self.sample_len

    @property
    def gw(self):
        return self.hg * A_HEAD_DIM

    @property
    def a_width(self):
        return len(DIL_GROUPS) * self.gw

    @property
    def ff_pad(self):
        return -(-self.ff // 1024) * 1024 if self.ff > 1024 else -(-self.ff // LANES) * LANES

    @property
    def alpha(self):
        return (2.0 * self.depth) ** 0.25


class LayerWeight(NamedTuple):
    stack: jax.Array
    layer: int


def _tile(n, pref):
    if n <= pref:
        return n
    t = (pref // LANES) * LANES
    while n % t:
        t -= LANES
    return t


def _params(sem):
    return pltpu.CompilerParams(dimension_semantics=sem,
                                vmem_limit_bytes=VMEM_LIMIT_MB * 1024 * 1024)


def _matmul(a, bs, extras, out_defs, epilogue, *, tm, tn, name, n_col_blocks=None, b_block=None,
            scratch=(), col_split=1):
    m, k_dim = a.shape
    nj = bs[0].stack.shape[2] // tn if n_col_blocks is None else n_col_blocks
    b_block = (lambda j: j) if b_block is None else b_block
    nb, ne, no = len(bs), len(extras), len(out_defs)
    swap = lambda f: (lambda j, i: f(i, j))

    a_spec = pl.BlockSpec((tm, k_dim), lambda j, i: (i, 0))
    b_specs = [pl.BlockSpec((None, k_dim, tn), lambda j, i, layer=b.layer: (layer, 0, b_block(j)))
               for b in bs]
    in_specs = [a_spec] + b_specs + [pl.BlockSpec(bshape, swap(f)) for _, bshape, f in extras]
    out_specs = [pl.BlockSpec(d[2], swap(d[3] if len(d) > 3 else (lambda i, j: (i, j)))) for d in out_defs]
    out_shape = [jax.ShapeDtypeStruct(d[0], d[1]) for d in out_defs]

    def kernel(*refs):
        a_ref = refs[0]
        b_refs = refs[1:1 + nb]
        e_refs = refs[1 + nb:1 + nb + ne]
        o_refs = refs[1 + nb + ne:1 + nb + ne + no]
        s_refs = refs[1 + nb + ne + no:]
        width = tn // col_split
        for s in range(col_split):
            cs = slice(s * width, (s + 1) * width)
            accs = [jnp.dot(a_ref[...], b[:, cs], preferred_element_type=F32) for b in b_refs]
            epilogue(accs, e_refs, o_refs, pl.program_id(0), s_refs, cs)

    return pl.pallas_call(
        kernel, grid=(nj, m // tm), in_specs=in_specs, out_specs=out_specs, out_shape=out_shape,
        scratch_shapes=list(scratch), compiler_params=_params(("parallel", "arbitrary")), name=name,
    )(a, *[b.stack for b in bs], *[e[0] for e in extras])


def _rows_out(m, n, dtype, tm, tn):
    return ((m, n), dtype, (tm, tn))


def _rope_lanes(x, c, sl, sr, half):
    return x * c + pltpu.roll(x, LANES - half, 1) * sl + pltpu.roll(x, half, 1) * sr


def _row_tables(tabs, tm):
    return [(t, (tm, LANES), lambda i, j: (i, 0)) for t in tabs]


def _proj_qkv_group(x_bf, w_qkv, tabs_a, g, dil, cfg):
    m, gw = cfg.m, cfg.gw
    tm = _tile(m, 1024)
    half = A_ROT_DIM // 2
    ngroups = len(DIL_GROUPS)

    def epilogue(accs, e, o, j, scr, cs):
        acc = accs[0]
        h0 = cs.start // LANES
        local = [slice(h * LANES, (h + 1) * LANES) for h in range((cs.stop - cs.start) // LANES)]

        def emit(piece):
            if dil == 1:
                for h, hs in enumerate(local):
                    lo = (h0 + h) * LANES
                    o[0][:, lo:lo + LANES] = piece(hs).astype(BF16)
                return
            for h, hs in enumerate(local):
                scr[0][h0 + h] = piece(hs)
            for r in range(dil):
                for h in range(len(local)):
                    lo = r * gw + (h0 + h) * LANES
                    o[0][:, lo:lo + LANES] = scr[0][h0 + h, pl.ds(r, tm // dil, stride=dil), :].astype(BF16)

        rot = j < 2
        c = jnp.where(rot, e[0][...], 1.0)
        sl = jnp.where(rot, e[1][...], 0.0)
        sr = jnp.where(rot, e[2][...], 0.0)
        emit(lambda hs: _rope_lanes(acc[:, hs], c, sl, sr, half))

    out_def = ((m // dil, 3 * dil * gw), BF16, (tm // dil, dil * gw))
    scratch = [pltpu.VMEM((gw // LANES, tm, LANES), F32)] if dil > 1 else []
    return _matmul(x_bf, [w_qkv], _row_tables(tabs_a, tm), [out_def], epilogue, tm=tm, tn=gw,
                   n_col_blocks=3, b_block=lambda j: j * ngroups + g, scratch=scratch,
                   col_split=max(1, gw // COL_SLAB), name=f"proj_qkv_g{g}")[0]


def _proj_latent(x_bf, w_lat, q_norm, kv_norm, tabs_b, cfg):
    m, ql, kl = cfg.m, cfg.q_lora, cfg.kv_lora
    n = ql + kl + LANES
    tm = _tile(m, 512)
    half = ROPE_DIM // 2

    def rms(v, g):
        return v * lax.rsqrt(jnp.mean(v * v, axis=-1, keepdims=True) + RMS_EPS) * g

    def epilogue(accs, e, o, j, scr, cs):
        acc = accs[0]
        o[0][...] = rms(acc[:, :ql], e[0][...]).astype(BF16)
        o[1][...] = rms(acc[:, ql:ql + kl], e[1][...]).astype(BF16)
        o[2][...] = _rope_lanes(acc[:, ql + kl:], e[2][...], e[3][...], e[4][...], half).astype(BF16)

    extras = [(q_norm.reshape(1, ql), (1, ql), lambda i, j: (0, 0)),
              (kv_norm.reshape(1, kl), (1, kl), lambda i, j: (0, 0))] + _row_tables(tabs_b, tm)
    outs = [_rows_out(m, ql, BF16, tm, ql), _rows_out(m, kl, BF16, tm, kl),
            _rows_out(m, LANES, BF16, tm, LANES)]
    return _matmul(x_bf, [w_lat], extras, outs, epilogue, tm=tm, tn=n, name="proj_latent")


def _proj_gates(x_bf, w_g, cfg):
    tm, tn = _tile(cfg.m, 1024), _tile(2 * cfg.d_model, 1024)

    def epilogue(accs, e, o, j, scr, cs):
        o[0][:, cs] = jax.nn.sigmoid(accs[0]).astype(BF16)

    return _matmul(x_bf, [w_g], [], [_rows_out(cfg.m, 2 * cfg.d_model, BF16, tm, tn)], epilogue,
                   tm=tm, tn=tn, col_split=tn // COL_SLAB, name="proj_gates")[0]


def _proj_qt(c_q, w_uq_t, cos_t, sin_t, cfg):
    m, ql = c_q.shape
    n = w_uq_t.stack.shape[1]
    heads_per_tile = min(4, cfg.b_heads)
    tn = heads_per_tile * QK_DIM
    tm = _tile(m, 2048)
    half = ROPE_DIM // 2
    scale = QK_DIM ** -0.5 * LOG2_E

    def kernel(w_ref, c_ref, cos_ref, sin_ref, o_ref):
        acc = lax.dot_general(w_ref[...], c_ref[...], (((1,), (1,)), ((), ())),
                              preferred_element_type=F32) * scale
        cos, sin = cos_ref[...], sin_ref[...]
        for h in range(heads_per_tile):
            lo = h * QK_DIM
            o_ref[lo:lo + NOPE_DIM, :] = acc[lo:lo + NOPE_DIM, :].astype(BF16)
            x1 = acc[lo + NOPE_DIM:lo + NOPE_DIM + half, :]
            x2 = acc[lo + NOPE_DIM + half:lo + QK_DIM, :]
            o_ref[lo + NOPE_DIM:lo + NOPE_DIM + half, :] = (x1 * cos - x2 * sin).astype(BF16)
            o_ref[lo + NOPE_DIM + half:lo + QK_DIM, :] = (x1 * sin + x2 * cos).astype(BF16)

    tab = pl.BlockSpec((half, tm), lambda j, i: (0, i))
    return pl.pallas_call(
        kernel, grid=(n // tn, m // tm),
        in_specs=[pl.BlockSpec((None, tn, ql), lambda j, i: (w_uq_t.layer, j, 0)),
                  pl.BlockSpec((tm, ql), lambda j, i: (i, 0)), tab, tab],
        out_specs=pl.BlockSpec((tn, tm), lambda j, i: (j, i)),
        out_shape=jax.ShapeDtypeStruct((n, m), BF16),
        compiler_params=_params(("parallel", "arbitrary")), name="proj_qt",
    )(w_uq_t.stack, c_q, cos_t, sin_t)


def _proj_kv(c_kv, w_ukv, cfg):
    m, h = cfg.m, cfg.b_heads
    hw = NOPE_DIM + V_DIM
    tm, tn = _tile(m, 2048), _tile(h * hw, 1024)
    heads = tn // hw

    def epilogue(accs, e, o, j, scr, cs):
        acc = accs[0]
        for hh in range(heads):
            o[0][hh] = acc[:, hh * hw:hh * hw + NOPE_DIM].astype(BF16)
            o[1][hh] = acc[:, hh * hw + NOPE_DIM:(hh + 1) * hw].T.astype(BF16)

    k_def = ((h, m, NOPE_DIM), BF16, (heads, tm, NOPE_DIM), lambda i, j: (j, i, 0))
    vt_def = ((h, V_DIM, m), BF16, (heads, V_DIM, tm), lambda i, j: (j, 0, i))
    return _matmul(c_kv, [w_ukv], [], [k_def, vt_def], epilogue, tm=tm, tn=tn, name="proj_kv")


def _branch_a(out_a, w_ba, gates, cfg):
    tm, tn = _tile(cfg.m, 1024), _tile(cfg.d_model, 512)

    def epilogue(accs, e, o, j, scr, cs):
        o[0][:, cs] = e[0][:, cs].astype(F32) * accs[0]

    extras = [(gates, (tm, tn), lambda i, j: (i, j))]
    return _matmul(out_a, [w_ba], extras, [_rows_out(cfg.m, cfg.d_model, F32, tm, tn)], epilogue,
                   tm=tm, tn=tn, col_split=tn // COL_SLAB, name="branch_a")[0]


def _branch_b_merge(out_b, w_bb, gates, part_a, cfg):
    tm, tn = _tile(cfg.m, 512), _tile(cfg.d_model, 512)
    goff = cfg.d_model // tn

    def epilogue(accs, e, o, j, scr, cs):
        o[0][:, cs] = (e[1][:, cs] + e[0][:, cs].astype(F32) * accs[0]).astype(BF16)

    extras = [(gates, (tm, tn), lambda i, j: (i, j + goff)),
              (part_a, (tm, tn), lambda i, j: (i, j))]
    return _matmul(out_b, [w_bb], extras, [_rows_out(cfg.m, cfg.d_model, BF16, tm, tn)], epilogue,
                   tm=tm, tn=tn, col_split=tn // COL_SLAB, name="branch_b_merge")[0]


def _residual_matmul(a, w, x_res, cfg, *, tm, name):
    tm, tn = _tile(cfg.m, tm), _tile(cfg.d_model, 512)
    alpha = cfg.alpha

    def epilogue(accs, e, o, j, scr, cs):
        o[0][:, cs] = alpha * e[0][:, cs] + accs[0]

    extras = [(x_res, (tm, tn), lambda i, j: (i, j))]
    return _matmul(a, [w], extras, [_rows_out(cfg.m, cfg.d_model, F32, tm, tn)], epilogue,
                   tm=tm, tn=tn, col_split=tn // COL_SLAB, name=name)[0]


def _swiglu(x_bf, w_gate, w_up, cfg):
    tm, tn = _tile(cfg.m, 1024), _tile(cfg.ff_pad, 512)

    def epilogue(accs, e, o, j, scr, cs):
        g, u = accs
        o[0][:, cs] = (g * jax.nn.sigmoid(g) * u).astype(BF16)

    return _matmul(x_bf, [w_gate, w_up], [], [_rows_out(cfg.m, cfg.ff_pad, BF16, tm, tn)], epilogue,
                   tm=tm, tn=tn, col_split=tn // COL_SLAB, name="swiglu")[0]


def _layer_norm(y, g, b, cfg):
    m, d = y.shape
    tm = _tile(m, 256)

    def kernel(y_ref, g_ref, b_ref, o_ref, obf_ref):
        v = y_ref[...]
        mu = jnp.mean(v, axis=-1, keepdims=True)
        vc = v - mu
        var = jnp.mean(vc * vc, axis=-1, keepdims=True)
        out = vc * lax.rsqrt(var + LN_EPS) * g_ref[...] + b_ref[...]
        o_ref[...] = out
        obf_ref[...] = out.astype(BF16)

    row = pl.BlockSpec((tm, d), lambda i: (i, 0))
    vec = pl.BlockSpec((1, d), lambda i: (0, 0))
    return pl.pallas_call(
        kernel, grid=(m // tm,), in_specs=[row, vec, vec], out_specs=[row, row],
        out_shape=[jax.ShapeDtypeStruct((m, d), F32), jax.ShapeDtypeStruct((m, d), BF16)],
        compiler_params=_params(("parallel",)), name="layer_norm",
    )(y, g.reshape(1, d), b.reshape(1, d))


def _final_layer_norm(y, g, b, cfg):
    m, d = y.shape
    tm = _tile(cfg.sample_len, 256)
    p_blocks = cfg.prompt_len // tm

    def kernel(y_ref, g_ref, b_ref, op_ref, os_ref):
        v = y_ref[...]
        mu = jnp.mean(v, axis=-1, keepdims=True)
        vc = v - mu
        var = jnp.mean(vc * vc, axis=-1, keepdims=True)
        out = vc * lax.rsqrt(var + LN_EPS) * g_ref[...] + b_ref[...]
        i = pl.program_id(0)

        @pl.when(i < p_blocks)
        def _():
            op_ref[...] = out

        @pl.when(i >= p_blocks)
        def _():
            os_ref[...] = out

    row = pl.BlockSpec((tm, d), lambda i: (i, 0))
    vec = pl.BlockSpec((1, d), lambda i: (0, 0))
    p_spec = pl.BlockSpec((tm, d), lambda i: (jnp.minimum(i, p_blocks - 1), 0))
    s_spec = pl.BlockSpec((tm, d), lambda i: (jnp.maximum(i - p_blocks, 0), 0))
    return pl.pallas_call(
        kernel, grid=(m // tm,), in_specs=[row, vec, vec], out_specs=[p_spec, s_spec],
        out_shape=[jax.ShapeDtypeStruct((cfg.prompt_len, d), F32),
                   jax.ShapeDtypeStruct((m - cfg.prompt_len, d), F32)],
        compiler_params=_params(("arbitrary",)), name="final_layer_norm",
    )(y, g.reshape(1, d), b.reshape(1, d))


def _segment(row0, p_len, s_len):
    in_prompt = row0 < p_len
    s_idx = jnp.maximum(row0 - p_len, 0) // s_len
    lo = jnp.where(in_prompt, 0, p_len + s_idx * s_len)
    return lo, jnp.where(in_prompt, p_len, s_len)


def _banded_attention(qkv_g, g, dil, cfg):
    hg, gw = cfg.hg, cfg.gw
    rows = cfg.m // dil
    p_len, s_len = cfg.prompt_len // dil, cfg.sample_len // dil
    bq = 2 * BAND_W
    kw = bq + 2 * BAND_W
    rc = _tile(rows, 2048)
    nsub = rc // bq
    scale = A_HEAD_DIM ** -0.5

    def kernel(q_ref, k_ref, v_ref, o_ref, lse_ref):
        base = pl.program_id(2) * rc

        def body(t, carry):
            s0 = pl.multiple_of(t * bq, bq)
            g0 = base + s0
            ks = pl.multiple_of(jnp.clip(g0 - BAND_W, 0, rows - kw), BAND_W)
            q = q_ref[pl.ds(s0, bq), :]
            k = k_ref[pl.ds(ks, kw), :]
            v = v_ref[pl.ds(ks, kw), :]
            s = lax.dot_general(q, k, (((1,), (1,)), ((), ())), preferred_element_type=F32) * scale
            qrow = g0 + lax.broadcasted_iota(jnp.int32, (bq, kw), 0)
            krow = ks + lax.broadcasted_iota(jnp.int32, (bq, kw), 1)
            lo, n = _segment(g0, p_len, s_len)
            valid = (jnp.abs(krow - qrow) <= BAND_W) & (krow >= lo) & (krow < lo + n)
            s = jnp.where(valid, s, NEG_BIG)
            mx = jnp.max(s, axis=-1, keepdims=True)
            p = jnp.exp(s - mx)
            l = jnp.sum(p, axis=-1, keepdims=True)
            pv = jnp.dot(p.astype(BF16), v, preferred_element_type=F32)
            o_ref[pl.ds(s0, bq), :] = pv / l
            lse_ref[pl.ds(s0, bq), :] = jnp.broadcast_to(mx + jnp.log(l), (bq, LANES))
            return carry

        lax.fori_loop(0, nsub, body, 0, unroll=BAND_UNROLL)

    def col(kind):
        return lambda r, j, c: (0, (kind * dil + r) * hg + j)

    q_spec = pl.BlockSpec((rc, LANES), lambda r, j, c: (c, r * hg + j))
    k_spec = pl.BlockSpec((rows, LANES), col(1))
    v_spec = pl.BlockSpec((rows, LANES), col(2))
    o_spec = pl.BlockSpec((rc, LANES), lambda r, j, c: (c, r * hg + j))
    shp = jax.ShapeDtypeStruct((rows, dil * gw), F32)
    return pl.pallas_call(
        kernel, grid=(dil, hg, rows // rc), in_specs=[q_spec, k_spec, v_spec],
        out_specs=[o_spec, o_spec], out_shape=[shp, shp],
        compiler_params=_params(("parallel", "parallel", "arbitrary")), name=f"banded_attention_g{g}",
    )(qkv_g, qkv_g, qkv_g)


def _mix_groups(outs, lses, cfg):
    m, gw = cfg.m, cfg.gw
    tm = _tile(m, 256)
    ng = len(outs)
    dils = [d for _, d in DIL_GROUPS]

    def kernel(*refs):
        o_refs, l_refs, out_ref = refs[:ng], refs[ng:2 * ng], refs[2 * ng]
        scr = list(refs[2 * ng + 1:])

        def token_major(ref, d):
            if d == 1:
                return ref[...]
            buf = scr.pop()
            for r in range(d):
                for h in range(cfg.hg):
                    lo = r * gw + h * LANES
                    buf[h, pl.ds(r, tm // d, stride=d), :] = ref[:, lo:lo + LANES]
            return jnp.concatenate([buf[h] for h in range(cfg.hg)], axis=1)

        os_ = [token_major(r, d) for r, d in zip(o_refs, dils)]
        ls = [token_major(r, d) for r, d in zip(l_refs, dils)]
        mx = functools.reduce(jnp.maximum, ls)
        es = [jnp.exp(l - mx) for l in ls]
        inv = 1.0 / functools.reduce(jnp.add, es)
        for gi in range(ng):
            out_ref[:, gi * gw:(gi + 1) * gw] = (os_[gi] * (es[gi] * inv)).astype(BF16)

    specs = [pl.BlockSpec((tm // d, d * gw), lambda i: (i, 0)) for d in dils]
    n_scr = 2 * sum(d > 1 for d in dils)
    return pl.pallas_call(
        kernel, grid=(m // tm,), in_specs=specs * 2,
        out_specs=pl.BlockSpec((tm, ng * gw), lambda i: (i, 0)),
        out_shape=jax.ShapeDtypeStruct((m, ng * gw), BF16),
        scratch_shapes=[pltpu.VMEM((cfg.hg, tm, LANES), F32)] * n_scr,
        compiler_params=_params(("parallel",)), name="mix_groups",
    )(*outs, *lses)


def _latent_attention(qt, kn, kpe, vt, cfg):
    m, h = cfg.m, cfg.b_heads
    p_len, s_len = cfg.prompt_len, cfg.sample_len
    tq = _tile(s_len, 512)
    tk = _tile(s_len // 4, 512)
    assert s_len % (4 * tk) == 0 and p_len % (4 * tk) == 0, "each sequence needs >= 2 chunk pairs"

    nq = 4
    assert s_len % (nq * tq) == 0 and p_len % (nq * tq) == 0
    assert m % tk == 0

    def kernel(q_ref, kn_ref, kpe_ref, vt_ref, o_ref, *scratch_refs):
        row0 = pl.program_id(1) * (nq * tq)
        lo, n = _segment(row0, p_len, s_len)
        npair = n // (2 * tk)
        streams = [scratch_refs[5 * u:5 * u + 5] for u in range(nq)]
        qts = [q_ref[:, u * tq:(u + 1) * tq] for u in range(nq)]

        def scores(u, c, s_ref):
            ks = pl.multiple_of(lo + c * tk, tk)
            k = jnp.concatenate([kn_ref[pl.ds(ks, tk), :], kpe_ref[pl.ds(ks, tk), :][:, :ROPE_DIM]],
                                axis=1)
            s = jnp.dot(k, qts[u], preferred_element_type=F32)
            s_ref[...] = s
            return jnp.max(s, axis=0, keepdims=True)

        def softmax(s_ref, p_ref, mx, chunk_max):
            mx_new = jnp.maximum(mx, chunk_max)
            p_ref[...] = jnp.exp2((s_ref[...] - mx_new).astype(BF16))
            return mx_new, jnp.exp2(mx - mx_new)

        ones_rows = jnp.ones((SUM_ROWS, tk), BF16)

        def values(c, p_ref, acc_ref, corr):
            ks = pl.multiple_of(lo + c * tk, tk)
            lhs = jnp.concatenate([vt_ref[:, pl.ds(ks, tk)], ones_rows], axis=0)
            acc_ref[...] = corr * acc_ref[...] + jnp.dot(lhs, p_ref[...], preferred_element_type=F32)

        def pair(i, carry, first=False, last=False):
            c = 2 * i
            us = range(nq)
            s_a, s_b, p_a, p_b, acc = zip(*streams)
            mx, max_a, corr_prev = (list(t) for t in zip(*carry))
            corr_a, corr_b = [None] * nq, [None] * nq
            max_b = [scores(u, c + 1, s_b[u]) for u in us]
            for u in us:
                mx[u], corr_a[u] = softmax(s_a[u], p_a[u], mx[u], max_a[u])
                if not first:
                    values(c - 1, p_b[u], acc[u], corr_prev[u])
            if not last:
                max_a = [scores(u, c + 2, s_a[u]) for u in us]
            for u in us:
                values(c, p_a[u], acc[u], corr_a[u])
                mx[u], corr_b[u] = softmax(s_b[u], p_b[u], mx[u], max_b[u])
            return tuple(zip(mx, max_a, corr_b))

        row = jnp.zeros((1, tq), F32)
        init = []
        for u, (s_a, _, _, _, acc_ref) in enumerate(streams):
            acc_ref[...] = jnp.zeros_like(acc_ref)
            init.append((row - jnp.inf, scores(u, 0, s_a), row))
        carry = pair(0, tuple(init), first=True)
        carry = lax.fori_loop(1, npair - 1, pair, carry)
        carry = pair(npair - 1, carry, last=True)
        for u, (_, _, _, p_b, acc_ref) in enumerate(streams):
            values(2 * npair - 1, p_b, acc_ref, carry[u][2])
            acc = acc_ref[...]
            o_ref[u * tq:(u + 1) * tq, :] = (acc[:V_DIM] / acc[V_DIM:V_DIM + 1]).T.astype(BF16)

    scratch = [pltpu.VMEM((tk, tq), F32), pltpu.VMEM((tk, tq), F32), pltpu.VMEM((tk, tq), BF16),
               pltpu.VMEM((tk, tq), BF16), pltpu.VMEM((V_DIM + SUM_ROWS, tq), F32)] * nq
    q_spec = pl.BlockSpec((QK_DIM, nq * tq), lambda hh, i: (hh, i))
    kn_spec = pl.BlockSpec((None, m, NOPE_DIM), lambda hh, i: (hh, 0, 0))
    kpe_spec = pl.BlockSpec((m, LANES), lambda hh, i: (0, 0))
    vt_spec = pl.BlockSpec((None, V_DIM, m), lambda hh, i: (hh, 0, 0))
    o_spec = pl.BlockSpec((nq * tq, V_DIM), lambda hh, i: (i, hh))
    return pl.pallas_call(
        kernel, grid=(h, m // (nq * tq)), in_specs=[q_spec, kn_spec, kpe_spec, vt_spec], out_specs=o_spec,
        out_shape=jax.ShapeDtypeStruct((m, h * V_DIM), BF16), scratch_shapes=scratch,
        compiler_params=_params(("parallel", "arbitrary")), name="latent_attention",
    )(qt, kn, kpe, vt)


def _rope_angles(pos, rot_dim):
    half = rot_dim // 2
    inv_freq = ROPE_THETA ** (-jnp.arange(half, dtype=F32) / half)
    return pos[:, None] * inv_freq[None, :]


def _rope_tables(pos, rot_dim):
    ang = _rope_angles(pos, rot_dim)
    cos, sin = jnp.cos(ang), jnp.sin(ang)
    rest = LANES - rot_dim
    ones = jnp.ones((pos.shape[0], rest), F32)
    zeros = jnp.zeros((pos.shape[0], rest), F32)
    zh = jnp.zeros_like(sin)
    c = jnp.concatenate([cos, cos, ones], axis=1)
    sl = jnp.concatenate([-sin, zh, zeros], axis=1)
    sr = jnp.concatenate([zh, sin, zeros], axis=1)
    return c, sl, sr


def _prep_weights(cfg, w_in, w_uq, w_ukv, w_ba, w_bb, w_o, w_gate, w_up, w_down):
    aw, ql, kl = cfg.a_width, cfg.q_lora, cfg.kv_lora
    off_cq = 3 * aw
    off_ga = off_cq + ql + kl + ROPE_DIM
    fpad = cfg.ff_pad - cfg.ff
    return dict(
        w_qkv=w_in[:, :, :off_cq].astype(BF16),
        w_lat=jnp.pad(w_in[:, :, off_cq:off_ga], ((0, 0), (0, 0), (0, LANES - ROPE_DIM))).astype(BF16),
        w_g=w_in[:, :, off_ga:].astype(BF16),
        w_uq_t=jnp.swapaxes(w_uq, 1, 2).astype(BF16), w_ukv=w_ukv.astype(BF16),
        w_ba=w_ba.astype(BF16), w_bb=w_bb.astype(BF16), w_o=w_o.astype(BF16),
        w_gate=jnp.pad(w_gate, ((0, 0), (0, 0), (0, fpad))).astype(BF16),
        w_up=jnp.pad(w_up, ((0, 0), (0, 0), (0, fpad))).astype(BF16),
        w_down=jnp.pad(w_down, ((0, 0), (0, fpad), (0, 0))).astype(BF16),
    )


def _layer(x, x_bf, w, q_norm, kv_norm, ln1_g, ln1_b, ln2_g, ln2_b, tabs_a, tabs_b, tabs_bt, cfg, last):
    c_q, c_kv, kpe = _proj_latent(x_bf, w["w_lat"], q_norm, kv_norm, tabs_b, cfg)
    gates = _proj_gates(x_bf, w["w_g"], cfg)

    outs, lses = [], []
    for g, (_, dil) in enumerate(DIL_GROUPS):
        qkv_g = _proj_qkv_group(x_bf, w["w_qkv"], tabs_a, g, dil, cfg)
        o, s = _banded_attention(qkv_g, g, dil, cfg)
        outs.append(o)
        lses.append(s)
    out_a = _mix_groups(outs, lses, cfg)

    qt = _proj_qt(c_q, w["w_uq_t"], *tabs_bt, cfg)
    kn, vt = _proj_kv(c_kv, w["w_ukv"], cfg)
    out_b = _latent_attention(qt, kn, kpe, vt, cfg)

    part_a = _branch_a(out_a, w["w_ba"], gates, cfg)
    merged = _branch_b_merge(out_b, w["w_bb"], gates, part_a, cfg)
    y1 = _residual_matmul(merged, w["w_o"], x, cfg, tm=1024, name="out_proj")
    x1, x1_bf = _layer_norm(y1, ln1_g, ln1_b, cfg)

    hmid = _swiglu(x1_bf, w["w_gate"], w["w_up"], cfg)
    y2 = _residual_matmul(hmid, w["w_down"], x1, cfg, tm=256, name="ffn_down")
    return (_final_layer_norm if last else _layer_norm)(y2, ln2_g, ln2_b, cfg)


def _trunk(cfg, x_prompt, x_sample, w_in, mla_q_norm, w_uq, mla_kv_norm, w_ukv, w_branch_a, w_branch_b,
           w_out, ln1_g, ln1_b, w_ffn_gate, w_ffn_up, w_ffn_down, ln2_g, ln2_b):
    d = cfg.d_model
    x = jnp.concatenate([x_prompt.reshape(-1, d), x_sample.reshape(-1, d)], axis=0)
    x_bf = x.astype(BF16)
    pos = jnp.concatenate([jnp.arange(cfg.prompt_len, dtype=F32),
                           jnp.tile(jnp.arange(cfg.sample_len, dtype=F32), cfg.n_sample)])
    tabs_a = _rope_tables(pos, A_ROT_DIM)
    tabs_b = _rope_tables(pos, ROPE_DIM)
    ang_t = _rope_angles(pos, ROPE_DIM).T
    tabs_bt = (jnp.cos(ang_t), jnp.sin(ang_t))
    stacks = _prep_weights(cfg, w_in, w_uq, w_ukv, w_branch_a, w_branch_b, w_out, w_ffn_gate, w_ffn_up,
                           w_ffn_down)
    for l in range(cfg.depth):
        w = {name: LayerWeight(stack, l) for name, stack in stacks.items()}
        last = l == cfg.depth - 1
        out = _layer(x, x_bf, w, mla_q_norm[l], mla_kv_norm[l], ln1_g[l], ln1_b[l], ln2_g[l],
                     ln2_b[l], tabs_a, tabs_b, tabs_bt, cfg, last)
        if not last:
            x, x_bf = out
    y_prompt, y_sample = out
    return (y_prompt.reshape(x_prompt.shape), y_sample.reshape(x_sample.shape))


def kernel(x_prompt, x_sample, w_in, mla_q_norm, w_uq, mla_kv_norm, w_ukv, w_branch_a, w_branch_b, w_out,
           ln1_g, ln1_b, w_ffn_gate, w_ffn_up, w_ffn_down, ln2_g, ln2_b):
    assert x_prompt.shape[0] == 1, "one prompt sequence"
    cfg = Cfg(d_model=x_prompt.shape[-1], prompt_len=x_prompt.shape[1], sample_len=x_sample.shape[1],
              n_sample=x_sample.shape[0], depth=w_in.shape[0],
              hg=w_branch_a.shape[1] // (len(DIL_GROUPS) * A_HEAD_DIM),
              b_heads=w_branch_b.shape[1] // V_DIM, q_lora=w_uq.shape[1], kv_lora=w_ukv.shape[1],
              ff=w_ffn_gate.shape[2])
    return _trunk(cfg, x_prompt, x_sample, w_in, mla_q_norm, w_uq, mla_kv_norm, w_ukv, w_branch_a,
                  w_branch_b, w_out, ln1_g, ln1_b, w_ffn_gate, w_ffn_up, w_ffn_down, ln2_g, ln2_b)
```

```python
import functools
from typing import NamedTuple

import jax
import jax.numpy as jnp
from jax import lax
from jax.experimental import pallas as pl
from jax.experimental.pallas import tpu as pltpu

F32 = jnp.float32
BF16 = jnp.bfloat16

LANES = 128
BF16_ROWS = 16
ROPE_THETA = 500000.0
LN_EPS = 1e-5
RMS_EPS = 1e-6
NEG_BIG = -1e30
LOG2_E = 1.4426950408889634

DIL_GROUPS = ((128, 1), (512, 4), (2048, 16))
BAND_W = 64
BAND_UNROLL = 8
A_HEAD_DIM = 128
A_ROT_DIM = 32
NOPE_DIM = 128
ROPE_DIM = 64
V_DIM = 128
QK_DIM = NOPE_DIM + ROPE_DIM
SUM_ROWS = BF16_ROWS
STREAM_GROUP = 2
COL_SLAB = 256
VMEM_LIMIT_MB = 56


class Cfg(NamedTuple):
    d_model: int
    prompt_len: int
    sample_len: int
    n_sample: int
    depth: int
    hg: int
    b_heads: int
    q_lora: int
    kv_lora: int
    ff: int

    @property
    def m(self):
        return self.prompt_len + self.n_sample * self.sample_len

    @property
    def gw(self):
        return self.hg * A_HEAD_DIM

    @property
    def a_width(self):
        return len(DIL_GROUPS) * self.gw

    @property
    def ff_pad(self):
        return -(-self.ff // 1024) * 1024 if self.ff > 1024 else -(-self.ff // LANES) * LANES

    @property
    def alpha(self):
        return (2.0 * self.depth) ** 0.25


class LayerWeight(NamedTuple):
    stack: jax.Array
    layer: int


def _tile(n, pref):
    if n <= pref:
        return n
    t = (pref // LANES) * LANES
    while n % t:
        t -= LANES
    return t


def _params(sem):
    return pltpu.CompilerParams(dimension_semantics=sem,
                                vmem_limit_bytes=VMEM_LIMIT_MB * 1024 * 1024)


def _matmul(a, bs, extras, out_defs, epilogue, *, tm, tn, name, n_col_blocks=None, b_block=None,
            scratch=(), col_split=1):
    m, k_dim = a.shape
    nj = bs[0].stack.shape[2] // tn if n_col_blocks is None else n_col_blocks
    b_block = (lambda j: j) if b_block is None else b_block
    nb, ne, no = len(bs), len(extras), len(out_defs)
    swap = lambda f: (lambda j, i: f(i, j))

    a_spec = pl.BlockSpec((tm, k_dim), lambda j, i: (i, 0))
    b_specs = [pl.BlockSpec((None, k_dim, tn), lambda j, i, layer=b.layer: (layer, 0, b_block(j)))
               for b in bs]
    in_specs = [a_spec] + b_specs + [pl.BlockSpec(bshape, swap(f)) for _, bshape, f in extras]
    out_specs = [pl.BlockSpec(d[2], swap(d[3] if len(d) > 3 else (lambda i, j: (i, j)))) for d in out_defs]
    out_shape = [jax.ShapeDtypeStruct(d[0], d[1]) for d in out_defs]

    def kernel(*refs):
        a_ref = refs[0]
        b_refs = refs[1:1 + nb]
        e_refs = refs[1 + nb:1 + nb + ne]
        o_refs = refs[1 + nb + ne:1 + nb + ne + no]
        s_refs = refs[1 + nb + ne + no:]
        width = tn // col_split
        for s in range(col_split):
            cs = slice(s * width, (s + 1) * width)
            accs = [jnp.dot(a_ref[...], b[:, cs], preferred_element_type=F32) for b in b_refs]
            epilogue(accs, e_refs, o_refs, pl.program_id(0), s_refs, cs)

    return pl.pallas_call(
        kernel, grid=(nj, m // tm), in_specs=in_specs, out_specs=out_specs, out_shape=out_shape,
        scratch_shapes=list(scratch), compiler_params=_params(("parallel", "arbitrary")), name=name,
    )(a, *[b.stack for b in bs], *[e[0] for e in extras])


def _rows_out(m, n, dtype, tm, tn):
    return ((m, n), dtype, (tm, tn))


def _rope_lanes(x, c, sl, sr, half):
    return x * c + pltpu.roll(x, LANES - half, 1) * sl + pltpu.roll(x, half, 1) * sr


def _row_tables(tabs, tm):
    return [(t, (tm, LANES), lambda i, j: (i, 0)) for t in tabs]


def _proj_qkv_group(x_bf, w_qkv, tabs_a, g, dil, cfg):
    m, gw = cfg.m, cfg.gw
    tm = _tile(m, 1024)
    half = A_ROT_DIM // 2
    ngroups = len(DIL_GROUPS)

    def epilogue(accs, e, o, j, scr, cs):
        acc = accs[0]
        h0 = cs.start // LANES
        local = [slice(h * LANES, (h + 1) * LANES) for h in range((cs.stop - cs.start) // LANES)]

        def emit(piece):
            if dil == 1:
                for h, hs in enumerate(local):
                    lo = (h0 + h) * LANES
                    o[0][:, lo:lo + LANES] = piece(hs).astype(BF16)
                return
            for h, hs in enumerate(local):
                scr[0][h0 + h] = piece(hs)
            for r in range(dil):
                for h in range(len(local)):
                    lo = r * gw + (h0 + h) * LANES
                    o[0][:, lo:lo + LANES] = scr[0][h0 + h, pl.ds(r, tm // dil, stride=dil), :].astype(BF16)

        rot = j < 2
        c = jnp.where(rot, e[0][...], 1.0)
        sl = jnp.where(rot, e[1][...], 0.0)
        sr = jnp.where(rot, e[2][...], 0.0)
        emit(lambda hs: _rope_lanes(acc[:, hs], c, sl, sr, half))

    out_def = ((m // dil, 3 * dil * gw), BF16, (tm // dil, dil * gw))
    scratch = [pltpu.VMEM((gw // LANES, tm, LANES), F32)] if dil > 1 else []
    return _matmul(x_bf, [w_qkv], _row_tables(tabs_a, tm), [out_def], epilogue, tm=tm, tn=gw,
                   n_col_blocks=3, b_block=lambda j: j * ngroups + g, scratch=scratch,
                   col_split=max(1, gw // COL_SLAB), name=f"proj_qkv_g{g}")[0]


def _proj_latent(x_bf, w_lat, q_norm, kv_norm, tabs_b, cfg):
    m, ql, kl = cfg.m, cfg.q_lora, cfg.kv_lora
    n = ql + kl + LANES
    tm = _tile(m, 512)
    half = ROPE_DIM // 2

    def rms(v, g):
        return v * lax.rsqrt(jnp.mean(v * v, axis=-1, keepdims=True) + RMS_EPS) * g

    def epilogue(accs, e, o, j, scr, cs):
        acc = accs[0]
        o[0][...] = rms(acc[:, :ql], e[0][...]).astype(BF16)
        o[1][...] = rms(acc[:, ql:ql + kl], e[1][...]).astype(BF16)
        o[2][...] = _rope_lanes(acc[:, ql + kl:], e[2][...], e[3][...], e[4][...], half).astype(BF16)

    extras = [(q_norm.reshape(1, ql), (1, ql), lambda i, j: (0, 0)),
              (kv_norm.reshape(1, kl), (1, kl), lambda i, j: (0, 0))] + _row_tables(tabs_b, tm)
    outs = [_rows_out(m, ql, BF16, tm, ql), _rows_out(m, kl, BF16, tm, kl),
            _rows_out(m, LANES, BF16, tm, LANES)]
    return _matmul(x_bf, [w_lat], extras, outs, epilogue, tm=tm, tn=n, name="proj_latent")


def _proj_gates(x_bf, w_g, cfg):
    tm, tn = _tile(cfg.m, 1024), _tile(2 * cfg.d_model, 1024)

    def epilogue(accs, e, o, j, scr, cs):
        o[0][:, cs] = jax.nn.sigmoid(accs[0]).astype(BF16)

    return _matmul(x_bf, [w_g], [], [_rows_out(cfg.m, 2 * cfg.d_model, BF16, tm, tn)], epilogue,
                   tm=tm, tn=tn, col_split=tn // COL_SLAB, name="proj_gates")[0]


def _proj_qt(c_q, w_uq_t, cos_t, sin_t, cfg):
    m, ql = c_q.shape
    n = w_uq_t.stack.shape[1]
    heads_per_tile = min(4, cfg.b_heads)
    tn = heads_per_tile * QK_DIM
    tm = _tile(m, 2048)
    half = ROPE_DIM // 2
    scale = QK_DIM ** -0.5 * LOG2_E

    def kernel(w_ref, c_ref, cos_ref, sin_ref, o_ref):
        acc = lax.dot_general(w_ref[...], c_ref[...], (((1,), (1,)), ((), ())),
                              preferred_element_type=F32) * scale
        cos, sin = cos_ref[...], sin_ref[...]
        for h in range(heads_per_tile):
            lo = h * QK_DIM
            o_ref[lo:lo + NOPE_DIM, :] = acc[lo:lo + NOPE_DIM, :].astype(BF16)
            x1 = acc[lo + NOPE_DIM:lo + NOPE_DIM + half, :]
            x2 = acc[lo + NOPE_DIM + half:lo + QK_DIM, :]
            o_ref[lo + NOPE_DIM:lo + NOPE_DIM + half, :] = (x1 * cos - x2 * sin).astype(BF16)
            o_ref[lo + NOPE_DIM + half:lo + QK_DIM, :] = (x1 * sin + x2 * cos).astype(BF16)

    tab = pl.BlockSpec((half, tm), lambda j, i: (0, i))
    return pl.pallas_call(
        kernel, grid=(n // tn, m // tm),
        in_specs=[pl.BlockSpec((None, tn, ql), lambda j, i: (w_uq_t.layer, j, 0)),
                  pl.BlockSpec((tm, ql), lambda j, i: (i, 0)), tab, tab],
        out_specs=pl.BlockSpec((tn, tm), lambda j, i: (j, i)),
        out_shape=jax.ShapeDtypeStruct((n, m), BF16),
        compiler_params=_params(("parallel", "arbitrary")), name="proj_qt",
    )(w_uq_t.stack, c_q, cos_t, sin_t)


def _proj_kv(c_kv, w_ukv, cfg):
    m, h = cfg.m, cfg.b_heads
    hw = NOPE_DIM + V_DIM
    tm, tn = _tile(m, 2048), _tile(h * hw, 1024)
    heads = tn // hw

    def epilogue(accs, e, o, j, scr, cs):
        acc = accs[0]
        for hh in range(heads):
            o[0][hh] = acc[:, hh * hw:hh * hw + NOPE_DIM].astype(BF16)
            o[1][hh] = acc[:, hh * hw + NOPE_DIM:(hh + 1) * hw].T.astype(BF16)

    k_def = ((h, m, NOPE_DIM), BF16, (heads, tm, NOPE_DIM), lambda i, j: (j, i, 0))
    vt_def = ((h, V_DIM, m), BF16, (heads, V_DIM, tm), lambda i, j: (j, 0, i))
    return _matmul(c_kv, [w_ukv], [], [k_def, vt_def], epilogue, tm=tm, tn=tn, name="proj_kv")


def _branch_a(out_a, w_ba, gates, cfg):
    tm, tn = _tile(cfg.m, 1024), _tile(cfg.d_model, 512)

    def epilogue(accs, e, o, j, scr, cs):
        o[0][:, cs] = e[0][:, cs].astype(F32) * accs[0]

    extras = [(gates, (tm, tn), lambda i, j: (i, j))]
    return _matmul(out_a, [w_ba], extras, [_rows_out(cfg.m, cfg.d_model, F32, tm, tn)], epilogue,
                   tm=tm, tn=tn, col_split=tn // COL_SLAB, name="branch_a")[0]


def _branch_b_merge(out_b, w_bb, gates, part_a, cfg):
    tm, tn = _tile(cfg.m, 512), _tile(cfg.d_model, 512)
    goff = cfg.d_model // tn

    def epilogue(accs, e, o, j, scr, cs):
        o[0][:, cs] = (e[1][:, cs] + e[0][:, cs].astype(F32) * accs[0]).astype(BF16)

    extras = [(gates, (tm, tn), lambda i, j: (i, j + goff)),
              (part_a, (tm, tn), lambda i, j: (i, j))]
    return _matmul(out_b, [w_bb], extras, [_rows_out(cfg.m, cfg.d_model, BF16, tm, tn)], epilogue,
                   tm=tm, tn=tn, col_split=tn // COL_SLAB, name="branch_b_merge")[0]


def _residual_matmul(a, w, x_res, cfg, *, tm, name):
    tm, tn = _tile(cfg.m, tm), _tile(cfg.d_model, 512)
    alpha = cfg.alpha

    def epilogue(accs, e, o, j, scr, cs):
        o[0][:, cs] = alpha * e[0][:, cs] + accs[0]

    extras = [(x_res, (tm, tn), lambda i, j: (i, j))]
    return _matmul(a, [w], extras, [_rows_out(cfg.m, cfg.d_model, F32, tm, tn)], epilogue,
                   tm=tm, tn=tn, col_split=tn // COL_SLAB, name=name)[0]


def _swiglu(x_bf, w_gate, w_up, cfg):
    tm, tn = _tile(cfg.m, 1024), _tile(cfg.ff_pad, 512)

    def epilogue(accs, e, o, j, scr, cs):
        g, u = accs
        o[0][:, cs] = (g * jax.nn.sigmoid(g) * u).astype(BF16)

    return _matmul(x_bf, [w_gate, w_up], [], [_rows_out(cfg.m, cfg.ff_pad, BF16, tm, tn)], epilogue,
                   tm=tm, tn=tn, col_split=tn // COL_SLAB, name="swiglu")[0]


def _layer_norm(y, g, b, cfg):
    m, d = y.shape
    tm = _tile(m, 256)

    def kernel(y_ref, g_ref, b_ref, o_ref, obf_ref):
        v = y_ref[...]
        mu = jnp.mean(v, axis=-1, keepdims=True)
        vc = v - mu
        var = jnp.mean(vc * vc, axis=-1, keepdims=True)
        out = vc * lax.rsqrt(var + LN_EPS) * g_ref[...] + b_ref[...]
        o_ref[...] = out
        obf_ref[...] = out.astype(BF16)

    row = pl.BlockSpec((tm, d), lambda i: (i, 0))
    vec = pl.BlockSpec((1, d), lambda i: (0, 0))
    return pl.pallas_call(
        kernel, grid=(m // tm,), in_specs=[row, vec, vec], out_specs=[row, row],
        out_shape=[jax.ShapeDtypeStruct((m, d), F32), jax.ShapeDtypeStruct((m, d), BF16)],
        compiler_params=_params(("parallel",)), name="layer_norm",
    )(y, g.reshape(1, d), b.reshape(1, d))


def _final_layer_norm(y, g, b, cfg):
    m, d = y.shape
    tm = _tile(cfg.sample_len, 256)
    p_blocks = cfg.prompt_len // tm

    def kernel(y_ref, g_ref, b_ref, op_ref, os_ref):
        v = y_ref[...]
        mu = jnp.mean(v, axis=-1, keepdims=True)
        vc = v - mu
        var = jnp.mean(vc * vc, axis=-1, keepdims=True)
        out = vc * lax.rsqrt(var + LN_EPS) * g_ref[...] + b_ref[...]
        i = pl.program_id(0)

        @pl.when(i < p_blocks)
        def _():
            op_ref[...] = out

        @pl.when(i >= p_blocks)
        def _():
            os_ref[...] = out

    row = pl.BlockSpec((tm, d), lambda i: (i, 0))
    vec = pl.BlockSpec((1, d), lambda i: (0, 0))
    p_spec = pl.BlockSpec((tm, d), lambda i: (jnp.minimum(i, p_blocks - 1), 0))
    s_spec = pl.BlockSpec((tm, d), lambda i: (jnp.maximum(i - p_blocks, 0), 0))
    return pl.pallas_call(
        kernel, grid=(m // tm,), in_specs=[row, vec, vec], out_specs=[p_spec, s_spec],
        out_shape=[jax.ShapeDtypeStruct((cfg.prompt_len, d), F32),
                   jax.ShapeDtypeStruct((m - cfg.prompt_len, d), F32)],
        compiler_params=_params(("arbitrary",)), name="final_layer_norm",
    )(y, g.reshape(1, d), b.reshape(1, d))


def _segment(row0, p_len, s_len):
    in_prompt = row0 < p_len
    s_idx = jnp.maximum(row0 - p_len, 0) // s_len
    lo = jnp.where(in_prompt, 0, p_len + s_idx * s_len)
    return lo, jnp.where(in_prompt, p_len, s_len)


def _banded_attention(qkv_g, g, dil, cfg):
    hg, gw = cfg.hg, cfg.gw
    rows = cfg.m // dil
    p_len, s_len = cfg.prompt_len // dil, cfg.sample_len // dil
    bq = 2 * BAND_W
    kw = bq + 2 * BAND_W
    rc = _tile(rows, 2048)
    nsub = rc // bq
    scale = A_HEAD_DIM ** -0.5

    def kernel(q_ref, k_ref, v_ref, o_ref, lse_ref):
        base = pl.program_id(2) * rc

        def body(t, carry):
            s0 = pl.multiple_of(t * bq, bq)
            g0 = base + s0
            ks = pl.multiple_of(jnp.clip(g0 - BAND_W, 0, rows - kw), BAND_W)
            q = q_ref[pl.ds(s0, bq), :]
            k = k_ref[pl.ds(ks, kw), :]
            v = v_ref[pl.ds(ks, kw), :]
            s = lax.dot_general(q, k, (((1,), (1,)), ((), ())), preferred_element_type=F32) * scale
            qrow = g0 + lax.broadcasted_iota(jnp.int32, (bq, kw), 0)
            krow = ks + lax.broadcasted_iota(jnp.int32, (bq, kw), 1)
            lo, n = _segment(g0, p_len, s_len)
            valid = (jnp.abs(krow - qrow) <= BAND_W) & (krow >= lo) & (krow < lo + n)
            s = jnp.where(valid, s, NEG_BIG)
            mx = jnp.max(s, axis=-1, keepdims=True)
            p = jnp.exp(s - mx)
            l = jnp.sum(p, axis=-1, keepdims=True)
            pv = jnp.dot(p.astype(BF16), v, preferred_element_type=F32)
            o_ref[pl.ds(s0, bq), :] = pv / l
            lse_ref[pl.ds(s0, bq), :] = jnp.broadcast_to(mx + jnp.log(l), (bq, LANES))
            return carry

        lax.fori_loop(0, nsub, body, 0, unroll=BAND_UNROLL)

    def col(kind):
        return lambda r, j, c: (0, (kind * dil + r) * hg + j)

    q_spec = pl.BlockSpec((rc, LANES), lambda r, j, c: (c, r * hg + j))
    k_spec = pl.BlockSpec((rows, LANES), col(1))
    v_spec = pl.BlockSpec((rows, LANES), col(2))
    o_spec = pl.BlockSpec((rc, LANES), lambda r, j, c: (c, r * hg + j))
    shp = jax.ShapeDtypeStruct((rows, dil * gw), F32)
    return pl.pallas_call(
        kernel, grid=(dil, hg, rows // rc), in_specs=[q_spec, k_spec, v_spec],
        out_specs=[o_spec, o_spec], out_shape=[shp, shp],
        compiler_params=_params(("parallel", "parallel", "arbitrary")), name=f"banded_attention_g{g}",
    )(qkv_g, qkv_g, qkv_g)


def _mix_groups(outs, lses, cfg):
    m, gw = cfg.m, cfg.gw
    tm = _tile(m, 256)
    ng = len(outs)
    dils = [d for _, d in DIL_GROUPS]

    def kernel(*refs):
        o_refs, l_refs, out_ref = refs[:ng], refs[ng:2 * ng], refs[2 * ng]
        scr = list(refs[2 * ng + 1:])

        def token_major(ref, d):
            if d == 1:
                return ref[...]
            buf = scr.pop()
            for r in range(d):
                for h in range(cfg.hg):
                    lo = r * gw + h * LANES
                    buf[h, pl.ds(r, tm // d, stride=d), :] = ref[:, lo:lo + LANES]
            return jnp.concatenate([buf[h] for h in range(cfg.hg)], axis=1)

        os_ = [token_major(r, d) for r, d in zip(o_refs, dils)]
        ls = [token_major(r, d) for r, d in zip(l_refs, dils)]
        mx = functools.reduce(jnp.maximum, ls)
        es = [jnp.exp(l - mx) for l in ls]
        inv = 1.0 / functools.reduce(jnp.add, es)
        for gi in range(ng):
            out_ref[:, gi * gw:(gi + 1) * gw] = (os_[gi] * (es[gi] * inv)).astype(BF16)

    specs = [pl.BlockSpec((tm // d, d * gw), lambda i: (i, 0)) for d in dils]
    n_scr = 2 * sum(d > 1 for d in dils)
    return pl.pallas_call(
        kernel, grid=(m // tm,), in_specs=specs * 2,
        out_specs=pl.BlockSpec((tm, ng * gw), lambda i: (i, 0)),
        out_shape=jax.ShapeDtypeStruct((m, ng * gw), BF16),
        scratch_shapes=[pltpu.VMEM((cfg.hg, tm, LANES), F32)] * n_scr,
        compiler_params=_params(("parallel",)), name="mix_groups",
    )(*outs, *lses)


def _latent_attention(qt, kn, kpe, vt, cfg):
    m, h = cfg.m, cfg.b_heads
    p_len, s_len = cfg.prompt_len, cfg.sample_len
    tq = _tile(s_len, 512)
    tk = _tile(s_len // 4, 512)
    assert s_len % (4 * tk) == 0 and p_len % (4 * tk) == 0, "each sequence needs >= 2 chunk pairs"

    nq = 4
    assert s_len % (nq * tq) == 0 and p_len % (nq * tq) == 0
    assert m % tk == 0

    def kernel(q_ref, kn_ref, kpe_ref, vt_ref, o_ref, *scratch_refs):
        row0 = pl.program_id(1) * (nq * tq)
        lo, n = _segment(row0, p_len, s_len)
        npair = n // (2 * tk)
        streams = [scratch_refs[5 * u:5 * u + 5] for u in range(nq)]
        qts = [q_ref[:, u * tq:(u + 1) * tq] for u in range(nq)]

        def scores(u, c, s_ref):
            ks = pl.multiple_of(lo + c * tk, tk)
            k = jnp.concatenate([kn_ref[pl.ds(ks, tk), :], kpe_ref[pl.ds(ks, tk), :][:, :ROPE_DIM]],
                                axis=1)
            s = jnp.dot(k, qts[u], preferred_element_type=F32)
            s_ref[...] = s
            return jnp.max(s, axis=0, keepdims=True)

        def softmax(s_ref, p_ref, mx, chunk_max):
            mx_new = jnp.maximum(mx, chunk_max)
            p_ref[...] = jnp.exp2((s_ref[...] - mx_new).astype(BF16))
            return mx_new, jnp.exp2(mx - mx_new)

        ones_rows = jnp.ones((SUM_ROWS, tk), BF16)

        def values(c, p_ref, acc_ref, corr):
            ks = pl.multiple_of(lo + c * tk, tk)
            lhs = jnp.concatenate([vt_ref[:, pl.ds(ks, tk)], ones_rows], axis=0)
            acc_ref[...] = corr * acc_ref[...] + jnp.dot(lhs, p_ref[...], preferred_element_type=F32)

        def pair(i, carry, first=False, last=False):
            c = 2 * i
            s_a, s_b, p_a, p_b, acc = zip(*streams)
            mx, max_a, corr_prev = (list(t) for t in zip(*carry))
            corr_a, corr_b, max_b = [None] * nq, [None] * nq, [None] * nq
            groups = [range(g0, min(g0 + STREAM_GROUP, nq)) for g0 in range(0, nq, STREAM_GROUP)]
            for us in groups:
                for u in us:
                    max_b[u] = scores(u, c + 1, s_b[u])
                for u in us:
                    mx[u], corr_a[u] = softmax(s_a[u], p_a[u], mx[u], max_a[u])
                    if not first:
                        values(c - 1, p_b[u], acc[u], corr_prev[u])
            for us in groups:
                if not last:
                    for u in us:
                        max_a[u] = scores(u, c + 2, s_a[u])
                for u in us:
                    values(c, p_a[u], acc[u], corr_a[u])
                    mx[u], corr_b[u] = softmax(s_b[u], p_b[u], mx[u], max_b[u])
            return tuple(zip(mx, max_a, corr_b))

        row = jnp.zeros((1, tq), F32)
        init = []
        for u, (s_a, _, _, _, acc_ref) in enumerate(streams):
            acc_ref[...] = jnp.zeros_like(acc_ref)
            init.append((row - jnp.inf, scores(u, 0, s_a), row))
        carry = pair(0, tuple(init), first=True)
        carry = lax.fori_loop(1, npair - 1, pair, carry)
        carry = pair(npair - 1, carry, last=True)
        for u, (_, _, _, p_b, acc_ref) in enumerate(streams):
            values(2 * npair - 1, p_b, acc_ref, carry[u][2])
            acc = acc_ref[...]
            o_ref[u * tq:(u + 1) * tq, :] = (acc[:V_DIM] / acc[V_DIM:V_DIM + 1]).T.astype(BF16)

    scratch = [pltpu.VMEM((tk, tq), F32), pltpu.VMEM((tk, tq), F32), pltpu.VMEM((tk, tq), BF16),
               pltpu.VMEM((tk, tq), BF16), pltpu.VMEM((V_DIM + SUM_ROWS, tq), F32)] * nq
    q_spec = pl.BlockSpec((QK_DIM, nq * tq), lambda hh, i: (hh, i))
    kn_spec = pl.BlockSpec((None, m, NOPE_DIM), lambda hh, i: (hh, 0, 0))
    kpe_spec = pl.BlockSpec((m, LANES), lambda hh, i: (0, 0))
    vt_spec = pl.BlockSpec((None, V_DIM, m), lambda hh, i: (hh, 0, 0))
    o_spec = pl.BlockSpec((nq * tq, V_DIM), lambda hh, i: (i, hh))
    return pl.pallas_call(
        kernel, grid=(h, m // (nq * tq)), in_specs=[q_spec, kn_spec, kpe_spec, vt_spec], out_specs=o_spec,
        out_shape=jax.ShapeDtypeStruct((m, h * V_DIM), BF16), scratch_shapes=scratch,
        compiler_params=_params(("parallel", "arbitrary")), name="latent_attention",
    )(qt, kn, kpe, vt)


def _rope_angles(pos, rot_dim):
    half = rot_dim // 2
    inv_freq = ROPE_THETA ** (-jnp.arange(half, dtype=F32) / half)
    return pos[:, None] * inv_freq[None, :]


def _rope_tables(pos, rot_dim):
    ang = _rope_angles(pos, rot_dim)
    cos, sin = jnp.cos(ang), jnp.sin(ang)
    rest = LANES - rot_dim
    ones = jnp.ones((pos.shape[0], rest), F32)
    zeros = jnp.zeros((pos.shape[0], rest), F32)
    zh = jnp.zeros_like(sin)
    c = jnp.concatenate([cos, cos, ones], axis=1)
    sl = jnp.concatenate([-sin, zh, zeros], axis=1)
    sr = jnp.concatenate([zh, sin, zeros], axis=1)
    return c, sl, sr


def _prep_weights(cfg, w_in, w_uq, w_ukv, w_ba, w_bb, w_o, w_gate, w_up, w_down):
    aw, ql, kl = cfg.a_width, cfg.q_lora, cfg.kv_lora
    off_cq = 3 * aw
    off_ga = off_cq + ql + kl + ROPE_DIM
    fpad = cfg.ff_pad - cfg.ff
    return dict(
        w_qkv=w_in[:, :, :off_cq].astype(BF16),
        w_lat=jnp.pad(w_in[:, :, off_cq:off_ga], ((0, 0), (0, 0), (0, LANES - ROPE_DIM))).astype(BF16),
        w_g=w_in[:, :, off_ga:].astype(BF16),
        w_uq_t=jnp.swapaxes(w_uq, 1, 2).astype(BF16), w_ukv=w_ukv.astype(BF16),
        w_ba=w_ba.astype(BF16), w_bb=w_bb.astype(BF16), w_o=w_o.astype(BF16),
        w_gate=jnp.pad(w_gate, ((0, 0), (0, 0), (0, fpad))).astype(BF16),
        w_up=jnp.pad(w_up, ((0, 0), (0, 0), (0, fpad))).astype(BF16),
        w_down=jnp.pad(w_down, ((0, 0), (0, fpad), (0, 0))).astype(BF16),
    )


def _layer(x, x_bf, w, q_norm, kv_norm, ln1_g, ln1_b, ln2_g, ln2_b, tabs_a, tabs_b, tabs_bt, cfg, last):
    c_q, c_kv, kpe = _proj_latent(x_bf, w["w_lat"], q_norm, kv_norm, tabs_b, cfg)
    gates = _proj_gates(x_bf, w["w_g"], cfg)

    outs, lses = [], []
    for g, (_, dil) in enumerate(DIL_GROUPS):
        qkv_g = _proj_qkv_group(x_bf, w["w_qkv"], tabs_a, g, dil, cfg)
        o, s = _banded_attention(qkv_g, g, dil, cfg)
        outs.append(o)
        lses.append(s)
    out_a = _mix_groups(outs, lses, cfg)

    qt = _proj_qt(c_q, w["w_uq_t"], *tabs_bt, cfg)
    kn, vt = _proj_kv(c_kv, w["w_ukv"], cfg)
    out_b = _latent_attention(qt, kn, kpe, vt, cfg)

    part_a = _branch_a(out_a, w["w_ba"], gates, cfg)
    merged = _branch_b_merge(out_b, w["w_bb"], gates, part_a, cfg)
    y1 = _residual_matmul(merged, w["w_o"], x, cfg, tm=1024, name="out_proj")
    x1, x1_bf = _layer_norm(y1, ln1_g, ln1_b, cfg)

    hmid = _swiglu(x1_bf, w["w_gate"], w["w_up"], cfg)
    y2 = _residual_matmul(hmid, w["w_down"], x1, cfg, tm=256, name="ffn_down")
    return (_final_layer_norm if last else _layer_norm)(y2, ln2_g, ln2_b, cfg)


def _trunk(cfg, x_prompt, x_sample, w_in, mla_q_norm, w_uq, mla_kv_norm, w_ukv, w_branch_a, w_branch_b,
           w_out, ln1_g, ln1_b, w_ffn_gate, w_ffn_up, w_ffn_down, ln2_g, ln2_b):
    d = cfg.d_model
    x = jnp.concatenate([x_prompt.reshape(-1, d), x_sample.reshape(-1, d)], axis=0)
    x_bf = x.astype(BF16)
    pos = jnp.concatenate([jnp.arange(cfg.prompt_len, dtype=F32),
                           jnp.tile(jnp.arange(cfg.sample_len, dtype=F32), cfg.n_sample)])
    tabs_a = _rope_tables(pos, A_ROT_DIM)
    tabs_b = _rope_tables(pos, ROPE_DIM)
    ang_t = _rope_angles(pos, ROPE_DIM).T
    tabs_bt = (jnp.cos(ang_t), jnp.sin(ang_t))
    stacks = _prep_weights(cfg, w_in, w_uq, w_ukv, w_branch_a, w_branch_b, w_out, w_ffn_gate, w_ffn_up,
                           w_ffn_down)
    for l in range(cfg.depth):
        w = {name: LayerWeight(stack, l) for name, stack in stacks.items()}
        last = l == cfg.depth - 1
        out = _layer(x, x_bf, w, mla_q_norm[l], mla_kv_norm[l], ln1_g[l], ln1_b[l], ln2_g[l],
                     ln2_b[l], tabs_a, tabs_b, tabs_bt, cfg, last)
        if not last:
            x, x_bf = out
    y_prompt, y_sample = out
    return (y_prompt.reshape(x_prompt.shape), y_sample.reshape(x_sample.shape))


def kernel(x_prompt, x_sample, w_in, mla_q_norm, w_uq, mla_kv_norm, w_ukv, w_branch_a, w_branch_b, w_out,
           ln1_g, ln1_b, w_ffn_gate, w_ffn_up, w_ffn_down, ln2_g, ln2_b):
    assert x_prompt.shape[0] == 1, "one prompt sequence"
    cfg = Cfg(d_model=x_prompt.shape[-1], prompt_len=x_prompt.shape[1], sample_len=x_sample.shape[1],
              n_sample=x_sample.shape[0], depth=w_in.shape[0],
              hg=w_branch_a.shape[1] // (len(DIL_GROUPS) * A_HEAD_DIM),
              b_heads=w_branch_b.shape[1] // V_DIM, q_lora=w_uq.shape[1], kv_lora=w_ukv.shape[1],
              ff=w_ffn_gate.shape[2])
    return _trunk(cfg, x_prompt, x_sample, w_in, mla_q_norm, w_uq, mla_kv_norm, w_ukv, w_branch_a,
                  w_branch_b, w_out, ln1_g, ln1_b, w_ffn_gate, w_ffn_up, w_ffn_down, ln2_g, ln2_b)
```

```python
import functools
from typing import NamedTuple

import jax
import jax.numpy as jnp
from jax import lax
from jax.experimental import pallas as pl
from jax.experimental.pallas import tpu as pltpu

F32 = jnp.float32
BF16 = jnp.bfloat16

LANES = 128
BF16_ROWS = 16
ROPE_THETA = 500000.0
LN_EPS = 1e-5
RMS_EPS = 1e-6
NEG_BIG = -1e30
LOG2_E = 1.4426950408889634

DIL_GROUPS = ((128, 1), (512, 4), (2048, 16))
BAND_W = 64
BAND_UNROLL = 8
A_HEAD_DIM = 128
A_ROT_DIM = 32
NOPE_DIM = 128
ROPE_DIM = 64
V_DIM = 128
QK_DIM = NOPE_DIM + ROPE_DIM
SUM_ROWS = BF16_ROWS
STREAM_GROUP = 2
COL_SLAB = 256
VMEM_LIMIT_MB = 56


class Cfg(NamedTuple):
    d_model: int
    prompt_len: int
    sample_len: int
    n_sample: int
    depth: int
    hg: int
    b_heads: int
    q_lora: int
    kv_lora: int
    ff: int

    @property
    def m(self):
        return self.prompt_len + self.n_sample * self.sample_len

    @property
    def gw(self):
        return self.hg * A_HEAD_DIM

    @property
    def a_width(self):
        return len(DIL_GROUPS) * self.gw

    @property
    def ff_pad(self):
        return -(-self.ff // 1024) * 1024 if self.ff > 1024 else -(-self.ff // LANES) * LANES

    @property
    def alpha(self):
        return (2.0 * self.depth) ** 0.25


class LayerWeight(NamedTuple):
    stack: jax.Array
    layer: int


def _tile(n, pref):
    if n <= pref:
        return n
    t = (pref // LANES) * LANES
    while n % t:
        t -= LANES
    return t


def _params(sem):
    return pltpu.CompilerParams(dimension_semantics=sem,
                                vmem_limit_bytes=VMEM_LIMIT_MB * 1024 * 1024)


def _matmul(a, bs, extras, out_defs, epilogue, *, tm, tn, name, n_col_blocks=None, b_block=None,
            scratch=(), col_split=1):
    nb, ne, no = len(bs), len(extras), len(out_defs)
    a_ops = list(a) if isinstance(a, (list, tuple)) else [a]
    na = len(a_ops)
    a_of = (lambda k: k) if na > 1 else (lambda k: 0)
    m = a_ops[0].shape[0]
    nj = bs[0].stack.shape[2] // tn if n_col_blocks is None else n_col_blocks
    b_block = (lambda j: j) if b_block is None else b_block
    b_blocks = list(b_block) if isinstance(b_block, (list, tuple)) else [b_block] * nb
    swap = lambda f: (lambda j, i: f(i, j))

    a_specs = [pl.BlockSpec((tm, x.shape[1]), lambda j, i: (i, 0)) for x in a_ops]
    b_specs = [pl.BlockSpec((None, a_ops[a_of(k)].shape[1], tn),
                            lambda j, i, layer=b.layer, f=b_blocks[k]: (layer, 0, f(j)))
               for k, b in enumerate(bs)]
    in_specs = a_specs + b_specs + [pl.BlockSpec(bshape, swap(f)) for _, bshape, f in extras]
    out_specs = [pl.BlockSpec(d[2], swap(d[3] if len(d) > 3 else (lambda i, j: (i, j)))) for d in out_defs]
    out_shape = [jax.ShapeDtypeStruct(d[0], d[1]) for d in out_defs]

    def kernel(*refs):
        a_refs = refs[:na]
        b_refs = refs[na:na + nb]
        e_refs = refs[na + nb:na + nb + ne]
        o_refs = refs[na + nb + ne:na + nb + ne + no]
        s_refs = refs[na + nb + ne + no:]
        width = tn // col_split
        for s in range(col_split):
            cs = slice(s * width, (s + 1) * width)
            accs = [jnp.dot(a_refs[a_of(k)][...], b[:, cs], preferred_element_type=F32)
                    for k, b in enumerate(b_refs)]
            epilogue(accs, e_refs, o_refs, pl.program_id(0), s_refs, cs)

    return pl.pallas_call(
        kernel, grid=(nj, m // tm), in_specs=in_specs, out_specs=out_specs, out_shape=out_shape,
        scratch_shapes=list(scratch), compiler_params=_params(("parallel", "arbitrary")), name=name,
    )(*a_ops, *[b.stack for b in bs], *[e[0] for e in extras])


def _rows_out(m, n, dtype, tm, tn):
    return ((m, n), dtype, (tm, tn))


def _rope_lanes(x, c, sl, sr, half):
    return x * c + pltpu.roll(x, LANES - half, 1) * sl + pltpu.roll(x, half, 1) * sr


def _row_tables(tabs, tm):
    return [(t, (tm, LANES), lambda i, j: (i, 0)) for t in tabs]


def _proj_qkv_group(x_bf, w_qkv, tabs_a, g, dil, cfg):
    m, gw = cfg.m, cfg.gw
    tm = _tile(m, 1024)
    half = A_ROT_DIM // 2
    ngroups = len(DIL_GROUPS)

    def epilogue(accs, e, o, j, scr, cs):
        acc = accs[0]
        h0 = cs.start // LANES
        local = [slice(h * LANES, (h + 1) * LANES) for h in range((cs.stop - cs.start) // LANES)]

        def emit(piece):
            if dil == 1:
                for h, hs in enumerate(local):
                    lo = (h0 + h) * LANES
                    o[0][:, lo:lo + LANES] = piece(hs).astype(BF16)
                return
            for h, hs in enumerate(local):
                scr[0][h0 + h] = piece(hs)
            for r in range(dil):
                for h in range(len(local)):
                    lo = r * gw + (h0 + h) * LANES
                    o[0][:, lo:lo + LANES] = scr[0][h0 + h, pl.ds(r, tm // dil, stride=dil), :].astype(BF16)

        rot = j < 2
        c = jnp.where(rot, e[0][...], 1.0)
        sl = jnp.where(rot, e[1][...], 0.0)
        sr = jnp.where(rot, e[2][...], 0.0)
        emit(lambda hs: _rope_lanes(acc[:, hs], c, sl, sr, half))

    out_def = ((m // dil, 3 * dil * gw), BF16, (tm // dil, dil * gw))
    scratch = [pltpu.VMEM((gw // LANES, tm, LANES), F32)] if dil > 1 else []
    return _matmul(x_bf, [w_qkv], _row_tables(tabs_a, tm), [out_def], epilogue, tm=tm, tn=gw,
                   n_col_blocks=3, b_block=lambda j: j * ngroups + g, scratch=scratch,
                   col_split=max(1, gw // COL_SLAB), name=f"proj_qkv_g{g}")[0]


def _proj_latent(x_bf, w_lat, q_norm, kv_norm, tabs_b, cfg):
    m, ql, kl = cfg.m, cfg.q_lora, cfg.kv_lora
    n = ql + kl + LANES
    tm = _tile(m, 512)
    half = ROPE_DIM // 2

    def rms(v, g):
        return v * lax.rsqrt(jnp.mean(v * v, axis=-1, keepdims=True) + RMS_EPS) * g

    def epilogue(accs, e, o, j, scr, cs):
        acc = accs[0]
        o[0][...] = rms(acc[:, :ql], e[0][...]).astype(BF16)
        o[1][...] = rms(acc[:, ql:ql + kl], e[1][...]).astype(BF16)
        o[2][...] = _rope_lanes(acc[:, ql + kl:], e[2][...], e[3][...], e[4][...], half).astype(BF16)

    extras = [(q_norm.reshape(1, ql), (1, ql), lambda i, j: (0, 0)),
              (kv_norm.reshape(1, kl), (1, kl), lambda i, j: (0, 0))] + _row_tables(tabs_b, tm)
    outs = [_rows_out(m, ql, BF16, tm, ql), _rows_out(m, kl, BF16, tm, kl),
            _rows_out(m, LANES, BF16, tm, LANES)]
    return _matmul(x_bf, [w_lat], extras, outs, epilogue, tm=tm, tn=n, name="proj_latent")


def _proj_qt(c_q, w_uq_t, cos_t, sin_t, cfg):
    m, ql = c_q.shape
    n = w_uq_t.stack.shape[1]
    heads_per_tile = min(4, cfg.b_heads)
    tn = heads_per_tile * QK_DIM
    tm = _tile(m, 2048)
    half = ROPE_DIM // 2
    scale = QK_DIM ** -0.5 * LOG2_E

    def kernel(w_ref, c_ref, cos_ref, sin_ref, o_ref):
        acc = lax.dot_general(w_ref[...], c_ref[...], (((1,), (1,)), ((), ())),
                              preferred_element_type=F32) * scale
        cos, sin = cos_ref[...], sin_ref[...]
        for h in range(heads_per_tile):
            lo = h * QK_DIM
            o_ref[lo:lo + NOPE_DIM, :] = acc[lo:lo + NOPE_DIM, :].astype(BF16)
            x1 = acc[lo + NOPE_DIM:lo + NOPE_DIM + half, :]
            x2 = acc[lo + NOPE_DIM + half:lo + QK_DIM, :]
            o_ref[lo + NOPE_DIM:lo + NOPE_DIM + half, :] = (x1 * cos - x2 * sin).astype(BF16)
            o_ref[lo + NOPE_DIM + half:lo + QK_DIM, :] = (x1 * sin + x2 * cos).astype(BF16)

    tab = pl.BlockSpec((half, tm), lambda j, i: (0, i))
    return pl.pallas_call(
        kernel, grid=(n // tn, m // tm),
        in_specs=[pl.BlockSpec((None, tn, ql), lambda j, i: (w_uq_t.layer, j, 0)),
                  pl.BlockSpec((tm, ql), lambda j, i: (i, 0)), tab, tab],
        out_specs=pl.BlockSpec((tn, tm), lambda j, i: (j, i)),
        out_shape=jax.ShapeDtypeStruct((n, m), BF16),
        compiler_params=_params(("parallel", "arbitrary")), name="proj_qt",
    )(w_uq_t.stack, c_q, cos_t, sin_t)


def _proj_kv(c_kv, w_ukv, cfg):
    m, h = cfg.m, cfg.b_heads
    hw = NOPE_DIM + V_DIM
    tm, tn = _tile(m, 2048), _tile(h * hw, 1024)
    heads = tn // hw

    def epilogue(accs, e, o, j, scr, cs):
        acc = accs[0]
        for hh in range(heads):
            o[0][hh] = acc[:, hh * hw:hh * hw + NOPE_DIM].astype(BF16)
            o[1][hh] = acc[:, hh * hw + NOPE_DIM:(hh + 1) * hw].T.astype(BF16)

    k_def = ((h, m, NOPE_DIM), BF16, (heads, tm, NOPE_DIM), lambda i, j: (j, i, 0))
    vt_def = ((h, V_DIM, m), BF16, (heads, V_DIM, tm), lambda i, j: (j, 0, i))
    return _matmul(c_kv, [w_ukv], [], [k_def, vt_def], epilogue, tm=tm, tn=tn, name="proj_kv")


def _branch_a(out_a, x_bf, w_ba, w_g, cfg):
    tm, tn = _tile(cfg.m, 512), _tile(cfg.d_model, 512)

    def epilogue(accs, e, o, j, scr, cs):
        o[0][:, cs] = jax.nn.sigmoid(accs[1]) * accs[0]

    return _matmul([out_a, x_bf], [w_ba, w_g], [], [_rows_out(cfg.m, cfg.d_model, F32, tm, tn)], epilogue,
                   tm=tm, tn=tn, n_col_blocks=cfg.d_model // tn, col_split=tn // COL_SLAB,
                   name="branch_a")[0]


def _branch_b_merge(out_b, x_bf, w_bb, w_g, part_a, cfg):
    tm, tn = _tile(cfg.m, 256), _tile(cfg.d_model, 512)
    goff = cfg.d_model // tn

    def epilogue(accs, e, o, j, scr, cs):
        o[0][:, cs] = (e[0][:, cs] + jax.nn.sigmoid(accs[1]) * accs[0]).astype(BF16)

    extras = [(part_a, (tm, tn), lambda i, j: (i, j))]
    return _matmul([out_b, x_bf], [w_bb, w_g], extras, [_rows_out(cfg.m, cfg.d_model, BF16, tm, tn)],
                   epilogue, tm=tm, tn=tn, n_col_blocks=cfg.d_model // tn,
                   b_block=[lambda j: j, lambda j: j + goff], col_split=tn // COL_SLAB,
                   name="branch_b_merge")[0]


def _residual_matmul(a, w, x_res, cfg, *, tm, name):
    tm, tn = _tile(cfg.m, tm), _tile(cfg.d_model, 512)
    alpha = cfg.alpha

    def epilogue(accs, e, o, j, scr, cs):
        o[0][:, cs] = alpha * e[0][:, cs] + accs[0]

    extras = [(x_res, (tm, tn), lambda i, j: (i, j))]
    return _matmul(a, [w], extras, [_rows_out(cfg.m, cfg.d_model, F32, tm, tn)], epilogue,
                   tm=tm, tn=tn, col_split=tn // COL_SLAB, name=name)[0]


def _swiglu(x_bf, w_gate, w_up, cfg):
    tm, tn = _tile(cfg.m, 1024), _tile(cfg.ff_pad, 512)

    def epilogue(accs, e, o, j, scr, cs):
        g, u = accs
        o[0][:, cs] = (g * jax.nn.sigmoid(g) * u).astype(BF16)

    return _matmul(x_bf, [w_gate, w_up], [], [_rows_out(cfg.m, cfg.ff_pad, BF16, tm, tn)], epilogue,
                   tm=tm, tn=tn, col_split=tn // COL_SLAB, name="swiglu")[0]


def _layer_norm(y, g, b, cfg):
    m, d = y.shape
    tm = _tile(m, 256)

    def kernel(y_ref, g_ref, b_ref, o_ref, obf_ref):
        v = y_ref[...]
        mu = jnp.mean(v, axis=-1, keepdims=True)
        vc = v - mu
        var = jnp.mean(vc * vc, axis=-1, keepdims=True)
        out = vc * lax.rsqrt(var + LN_EPS) * g_ref[...] + b_ref[...]
        o_ref[...] = out
        obf_ref[...] = out.astype(BF16)

    row = pl.BlockSpec((tm, d), lambda i: (i, 0))
    vec = pl.BlockSpec((1, d), lambda i: (0, 0))
    return pl.pallas_call(
        kernel, grid=(m // tm,), in_specs=[row, vec, vec], out_specs=[row, row],
        out_shape=[jax.ShapeDtypeStruct((m, d), F32), jax.ShapeDtypeStruct((m, d), BF16)],
        compiler_params=_params(("parallel",)), name="layer_norm",
    )(y, g.reshape(1, d), b.reshape(1, d))


def _final_layer_norm(y, g, b, cfg):
    m, d = y.shape
    tm = _tile(cfg.sample_len, 256)
    p_blocks = cfg.prompt_len // tm

    def kernel(y_ref, g_ref, b_ref, op_ref, os_ref):
        v = y_ref[...]
        mu = jnp.mean(v, axis=-1, keepdims=True)
        vc = v - mu
        var = jnp.mean(vc * vc, axis=-1, keepdims=True)
        out = vc * lax.rsqrt(var + LN_EPS) * g_ref[...] + b_ref[...]
        i = pl.program_id(0)

        @pl.when(i < p_blocks)
        def _():
            op_ref[...] = out

        @pl.when(i >= p_blocks)
        def _():
            os_ref[...] = out

    row = pl.BlockSpec((tm, d), lambda i: (i, 0))
    vec = pl.BlockSpec((1, d), lambda i: (0, 0))
    p_spec = pl.BlockSpec((tm, d), lambda i: (jnp.minimum(i, p_blocks - 1), 0))
    s_spec = pl.BlockSpec((tm, d), lambda i: (jnp.maximum(i - p_blocks, 0), 0))
    return pl.pallas_call(
        kernel, grid=(m // tm,), in_specs=[row, vec, vec], out_specs=[p_spec, s_spec],
        out_shape=[jax.ShapeDtypeStruct((cfg.prompt_len, d), F32),
                   jax.ShapeDtypeStruct((m - cfg.prompt_len, d), F32)],
        compiler_params=_params(("arbitrary",)), name="final_layer_norm",
    )(y, g.reshape(1, d), b.reshape(1, d))


def _segment(row0, p_len, s_len):
    in_prompt = row0 < p_len
    s_idx = jnp.maximum(row0 - p_len, 0) // s_len
    lo = jnp.where(in_prompt, 0, p_len + s_idx * s_len)
    return lo, jnp.where(in_prompt, p_len, s_len)


def _banded_attention(qkv_g, g, dil, cfg):
    hg, gw = cfg.hg, cfg.gw
    rows = cfg.m // dil
    p_len, s_len = cfg.prompt_len // dil, cfg.sample_len // dil
    bq = 2 * BAND_W
    kw = bq + 2 * BAND_W
    rc = _tile(rows, 2048)
    nsub = rc // bq
    scale = A_HEAD_DIM ** -0.5

    def kernel(q_ref, k_ref, v_ref, o_ref, lse_ref):
        base = pl.program_id(2) * rc

        def body(t, carry):
            s0 = pl.multiple_of(t * bq, bq)
            g0 = base + s0
            ks = pl.multiple_of(jnp.clip(g0 - BAND_W, 0, rows - kw), BAND_W)
            q = q_ref[pl.ds(s0, bq), :]
            k = k_ref[pl.ds(ks, kw), :]
            v = v_ref[pl.ds(ks, kw), :]
            s = lax.dot_general(q, k, (((1,), (1,)), ((), ())), preferred_element_type=F32) * scale
            qrow = g0 + lax.broadcasted_iota(jnp.int32, (bq, kw), 0)
            krow = ks + lax.broadcasted_iota(jnp.int32, (bq, kw), 1)
            lo, n = _segment(g0, p_len, s_len)
            valid = (jnp.abs(krow - qrow) <= BAND_W) & (krow >= lo) & (krow < lo + n)
            s = jnp.where(valid, s, NEG_BIG)
            mx = jnp.max(s, axis=-1, keepdims=True)
            p = jnp.exp(s - mx)
            l = jnp.sum(p, axis=-1, keepdims=True)
            pv = jnp.dot(p.astype(BF16), v, preferred_element_type=F32)
            o_ref[pl.ds(s0, bq), :] = pv / l
            lse_ref[pl.ds(s0, bq), :] = jnp.broadcast_to(mx + jnp.log(l), (bq, LANES))
            return carry

        lax.fori_loop(0, nsub, body, 0, unroll=BAND_UNROLL)

    def col(kind):
        return lambda r, j, c: (0, (kind * dil + r) * hg + j)

    q_spec = pl.BlockSpec((rc, LANES), lambda r, j, c: (c, r * hg + j))
    k_spec = pl.BlockSpec((rows, LANES), col(1))
    v_spec = pl.BlockSpec((rows, LANES), col(2))
    o_spec = pl.BlockSpec((rc, LANES), lambda r, j, c: (c, r * hg + j))
    shp = jax.ShapeDtypeStruct((rows, dil * gw), F32)
    return pl.pallas_call(
        kernel, grid=(dil, hg, rows // rc), in_specs=[q_spec, k_spec, v_spec],
        out_specs=[o_spec, o_spec], out_shape=[shp, shp],
        compiler_params=_params(("parallel", "parallel", "arbitrary")), name=f"banded_attention_g{g}",
    )(qkv_g, qkv_g, qkv_g)


def _mix_groups(outs, lses, cfg):
    m, gw = cfg.m, cfg.gw
    tm = _tile(m, 256)
    ng = len(outs)
    dils = [d for _, d in DIL_GROUPS]

    def kernel(*refs):
        o_refs, l_refs, out_ref = refs[:ng], refs[ng:2 * ng], refs[2 * ng]
        scr = list(refs[2 * ng + 1:])

        def token_major(ref, d):
            if d == 1:
                return ref[...]
            buf = scr.pop()
            for r in range(d):
                for h in range(cfg.hg):
                    lo = r * gw + h * LANES
                    buf[h, pl.ds(r, tm // d, stride=d), :] = ref[:, lo:lo + LANES]
            return jnp.concatenate([buf[h] for h in range(cfg.hg)], axis=1)

        os_ = [token_major(r, d) for r, d in zip(o_refs, dils)]
        ls = [token_major(r, d) for r, d in zip(l_refs, dils)]
        mx = functools.reduce(jnp.maximum, ls)
        es = [jnp.exp(l - mx) for l in ls]
        inv = 1.0 / functools.reduce(jnp.add, es)
        for gi in range(ng):
            out_ref[:, gi * gw:(gi + 1) * gw] = (os_[gi] * (es[gi] * inv)).astype(BF16)

    specs = [pl.BlockSpec((tm // d, d * gw), lambda i: (i, 0)) for d in dils]
    n_scr = 2 * sum(d > 1 for d in dils)
    return pl.pallas_call(
        kernel, grid=(m // tm,), in_specs=specs * 2,
        out_specs=pl.BlockSpec((tm, ng * gw), lambda i: (i, 0)),
        out_shape=jax.ShapeDtypeStruct((m, ng * gw), BF16),
        scratch_shapes=[pltpu.VMEM((cfg.hg, tm, LANES), F32)] * n_scr,
        compiler_params=_params(("parallel",)), name="mix_groups",
    )(*outs, *lses)


def _latent_attention(qt, kn, kpe, vt, cfg):
    m, h = cfg.m, cfg.b_heads
    p_len, s_len = cfg.prompt_len, cfg.sample_len
    tq = _tile(s_len, 512)
    tk = _tile(s_len // 4, 512)
    assert s_len % (4 * tk) == 0 and p_len % (4 * tk) == 0, "each sequence needs >= 2 chunk pairs"

    nq = 4
    assert s_len % (nq * tq) == 0 and p_len % (nq * tq) == 0
    assert m % tk == 0

    def kernel(q_ref, kn_ref, kpe_ref, vt_ref, o_ref, *scratch_refs):
        row0 = pl.program_id(1) * (nq * tq)
        lo, n = _segment(row0, p_len, s_len)
        npair = n // (2 * tk)
        streams = [scratch_refs[5 * u:5 * u + 5] for u in range(nq)]
        qts = [q_ref[:, u * tq:(u + 1) * tq] for u in range(nq)]

        def scores(u, c, s_ref):
            ks = pl.multiple_of(lo + c * tk, tk)
            k = jnp.concatenate([kn_ref[pl.ds(ks, tk), :], kpe_ref[pl.ds(ks, tk), :][:, :ROPE_DIM]],
                                axis=1)
            s = jnp.dot(k, qts[u], preferred_element_type=F32)
            s_ref[...] = s
            return jnp.max(s, axis=0, keepdims=True)

        def softmax(s_ref, p_ref, mx, chunk_max):
            mx_new = jnp.maximum(mx, chunk_max)
            p_ref[...] = jnp.exp2((s_ref[...] - mx_new).astype(BF16))
            return mx_new, jnp.exp2(mx - mx_new)

        ones_rows = jnp.ones((SUM_ROWS, tk), BF16)

        def values(c, p_ref, acc_ref, corr):
            ks = pl.multiple_of(lo + c * tk, tk)
            lhs = jnp.concatenate([vt_ref[:, pl.ds(ks, tk)], ones_rows], axis=0)
            acc_ref[...] = corr * acc_ref[...] + jnp.dot(lhs, p_ref[...], preferred_element_type=F32)

        def pair(i, carry, first=False, last=False):
            c = 2 * i
            s_a, s_b, p_a, p_b, acc = zip(*streams)
            mx, max_a, corr_prev = (list(t) for t in zip(*carry))
            corr_a, corr_b, max_b = [None] * nq, [None] * nq, [None] * nq
            groups = [range(g0, min(g0 + STREAM_GROUP, nq)) for g0 in range(0, nq, STREAM_GROUP)]
            for us in groups:
                for u in us:
                    max_b[u] = scores(u, c + 1, s_b[u])
                for u in us:
                    mx[u], corr_a[u] = softmax(s_a[u], p_a[u], mx[u], max_a[u])
                    if not first:
                        values(c - 1, p_b[u], acc[u], corr_prev[u])
            for us in groups:
                if not last:
                    for u in us:
                        max_a[u] = scores(u, c + 2, s_a[u])
                for u in us:
                    values(c, p_a[u], acc[u], corr_a[u])
                    mx[u], corr_b[u] = softmax(s_b[u], p_b[u], mx[u], max_b[u])
            return tuple(zip(mx, max_a, corr_b))

        row = jnp.zeros((1, tq), F32)
        init = []
        for u, (s_a, _, _, _, acc_ref) in enumerate(streams):
            acc_ref[...] = jnp.zeros_like(acc_ref)
            init.append((row - jnp.inf, scores(u, 0, s_a), row))
        carry = pair(0, tuple(init), first=True)
        carry = lax.fori_loop(1, npair - 1, pair, carry)
        carry = pair(npair - 1, carry, last=True)
        for u, (_, _, _, p_b, acc_ref) in enumerate(streams):
            values(2 * npair - 1, p_b, acc_ref, carry[u][2])
            acc = acc_ref[...]
            o_ref[u * tq:(u + 1) * tq, :] = (acc[:V_DIM] / acc[V_DIM:V_DIM + 1]).T.astype(BF16)

    scratch = [pltpu.VMEM((tk, tq), F32), pltpu.VMEM((tk, tq), F32), pltpu.VMEM((tk, tq), BF16),
               pltpu.VMEM((tk, tq), BF16), pltpu.VMEM((V_DIM + SUM_ROWS, tq), F32)] * nq
    q_spec = pl.BlockSpec((QK_DIM, nq * tq), lambda hh, i: (hh, i))
    kn_spec = pl.BlockSpec((None, m, NOPE_DIM), lambda hh, i: (hh, 0, 0))
    kpe_spec = pl.BlockSpec((m, LANES), lambda hh, i: (0, 0))
    vt_spec = pl.BlockSpec((None, V_DIM, m), lambda hh, i: (hh, 0, 0))
    o_spec = pl.BlockSpec((nq * tq, V_DIM), lambda hh, i: (i, hh))
    return pl.pallas_call(
        kernel, grid=(h, m // (nq * tq)), in_specs=[q_spec, kn_spec, kpe_spec, vt_spec], out_specs=o_spec,
        out_shape=jax.ShapeDtypeStruct((m, h * V_DIM), BF16), scratch_shapes=scratch,
        compiler_params=_params(("parallel", "arbitrary")), name="latent_attention",
    )(qt, kn, kpe, vt)


def _rope_angles(pos, rot_dim):
    half = rot_dim // 2
    inv_freq = ROPE_THETA ** (-jnp.arange(half, dtype=F32) / half)
    return pos[:, None] * inv_freq[None, :]


def _rope_tables(pos, rot_dim):
    ang = _rope_angles(pos, rot_dim)
    cos, sin = jnp.cos(ang), jnp.sin(ang)
    rest = LANES - rot_dim
    ones = jnp.ones((pos.shape[0], rest), F32)
    zeros = jnp.zeros((pos.shape[0], rest), F32)
    zh = jnp.zeros_like(sin)
    c = jnp.concatenate([cos, cos, ones], axis=1)
    sl = jnp.concatenate([-sin, zh, zeros], axis=1)
    sr = jnp.concatenate([zh, sin, zeros], axis=1)
    return c, sl, sr


def _prep_weights(cfg, w_in, w_uq, w_ukv, w_ba, w_bb, w_o, w_gate, w_up, w_down):
    aw, ql, kl = cfg.a_width, cfg.q_lora, cfg.kv_lora
    off_cq = 3 * aw
    off_ga = off_cq + ql + kl + ROPE_DIM
    fpad = cfg.ff_pad - cfg.ff
    return dict(
        w_qkv=w_in[:, :, :off_cq].astype(BF16),
        w_lat=jnp.pad(w_in[:, :, off_cq:off_ga], ((0, 0), (0, 0), (0, LANES - ROPE_DIM))).astype(BF16),
        w_g=w_in[:, :, off_ga:].astype(BF16),
        w_uq_t=jnp.swapaxes(w_uq, 1, 2).astype(BF16), w_ukv=w_ukv.astype(BF16),
        w_ba=w_ba.astype(BF16), w_bb=w_bb.astype(BF16), w_o=w_o.astype(BF16),
        w_gate=jnp.pad(w_gate, ((0, 0), (0, 0), (0, fpad))).astype(BF16),
        w_up=jnp.pad(w_up, ((0, 0), (0, 0), (0, fpad))).astype(BF16),
        w_down=jnp.pad(w_down, ((0, 0), (0, fpad), (0, 0))).astype(BF16),
    )


def _layer(x, x_bf, w, q_norm, kv_norm, ln1_g, ln1_b, ln2_g, ln2_b, tabs_a, tabs_b, tabs_bt, cfg, last):
    c_q, c_kv, kpe = _proj_latent(x_bf, w["w_lat"], q_norm, kv_norm, tabs_b, cfg)

    outs, lses = [], []
    for g, (_, dil) in enumerate(DIL_GROUPS):
        qkv_g = _proj_qkv_group(x_bf, w["w_qkv"], tabs_a, g, dil, cfg)
        o, s = _banded_attention(qkv_g, g, dil, cfg)
        outs.append(o)
        lses.append(s)
    out_a = _mix_groups(outs, lses, cfg)

    qt = _proj_qt(c_q, w["w_uq_t"], *tabs_bt, cfg)
    kn, vt = _proj_kv(c_kv, w["w_ukv"], cfg)
    out_b = _latent_attention(qt, kn, kpe, vt, cfg)

    part_a = _branch_a(out_a, x_bf, w["w_ba"], w["w_g"], cfg)
    merged = _branch_b_merge(out_b, x_bf, w["w_bb"], w["w_g"], part_a, cfg)
    y1 = _residual_matmul(merged, w["w_o"], x, cfg, tm=1024, name="out_proj")
    x1, x1_bf = _layer_norm(y1, ln1_g, ln1_b, cfg)

    hmid = _swiglu(x1_bf, w["w_gate"], w["w_up"], cfg)
    y2 = _residual_matmul(hmid, w["w_down"], x1, cfg, tm=256, name="ffn_down")
    return (_final_layer_norm if last else _layer_norm)(y2, ln2_g, ln2_b, cfg)


def _trunk(cfg, x_prompt, x_sample, w_in, mla_q_norm, w_uq, mla_kv_norm, w_ukv, w_branch_a, w_branch_b,
           w_out, ln1_g, ln1_b, w_ffn_gate, w_ffn_up, w_ffn_down, ln2_g, ln2_b):
    d = cfg.d_model
    x = jnp.concatenate([x_prompt.reshape(-1, d), x_sample.reshape(-1, d)], axis=0)
    x_bf = x.astype(BF16)
    pos = jnp.concatenate([jnp.arange(cfg.prompt_len, dtype=F32),
                           jnp.tile(jnp.arange(cfg.sample_len, dtype=F32), cfg.n_sample)])
    tabs_a = _rope_tables(pos, A_ROT_DIM)
    tabs_b = _rope_tables(pos, ROPE_DIM)
    ang_t = _rope_angles(pos, ROPE_DIM).T
    tabs_bt = (jnp.cos(ang_t), jnp.sin(ang_t))
    stacks = _prep_weights(cfg, w_in, w_uq, w_ukv, w_branch_a, w_branch_b, w_out, w_ffn_gate, w_ffn_up,
                           w_ffn_down)
    for l in range(cfg.depth):
        w = {name: LayerWeight(stack, l) for name, stack in stacks.items()}
        last = l == cfg.depth - 1
        out = _layer(x, x_bf, w, mla_q_norm[l], mla_kv_norm[l], ln1_g[l], ln1_b[l], ln2_g[l],
                     ln2_b[l], tabs_a, tabs_b, tabs_bt, cfg, last)
        if not last:
            x, x_bf = out
    y_prompt, y_sample = out
    return (y_prompt.reshape(x_prompt.shape), y_sample.reshape(x_sample.shape))


def kernel(x_prompt, x_sample, w_in, mla_q_norm, w_uq, mla_kv_norm, w_ukv, w_branch_a, w_branch_b, w_out,
           ln1_g, ln1_b, w_ffn_gate, w_ffn_up, w_ffn_down, ln2_g, ln2_b):
    assert x_prompt.shape[0] == 1, "one prompt sequence"
    cfg = Cfg(d_model=x_prompt.shape[-1], prompt_len=x_prompt.shape[1], sample_len=x_sample.shape[1],
              n_sample=x_sample.shape[0], depth=w_in.shape[0],
              hg=w_branch_a.shape[1] // (len(DIL_GROUPS) * A_HEAD_DIM),
              b_heads=w_branch_b.shape[1] // V_DIM, q_lora=w_uq.shape[1], kv_lora=w_ukv.shape[1],
              ff=w_ffn_gate.shape[2])
    return _trunk(cfg, x_prompt, x_sample, w_in, mla_q_norm, w_uq, mla_kv_norm, w_ukv, w_branch_a,
                  w_branch_b, w_out, ln1_g, ln1_b, w_ffn_gate, w_ffn_up, w_ffn_down, ln2_g, ln2_b)
```

```python
import functools
from typing import NamedTuple

import jax
import jax.numpy as jnp
from jax import lax
from jax.experimental import pallas as pl
from jax.experimental.pallas import tpu as pltpu

F32 = jnp.float32
BF16 = jnp.bfloat16

LANES = 128
BF16_ROWS = 16
ROPE_THETA = 500000.0
LN_EPS = 1e-5
RMS_EPS = 1e-6
NEG_BIG = -1e30
LOG2_E = 1.4426950408889634

DIL_GROUPS = ((128, 1), (512, 4), (2048, 16))
BAND_W = 64
BAND_UNROLL = 8
A_HEAD_DIM = 128
A_ROT_DIM = 32
NOPE_DIM = 128
ROPE_DIM = 64
V_DIM = 128
QK_DIM = NOPE_DIM + ROPE_DIM
SUM_ROWS = BF16_ROWS
STREAM_GROUP = 2
COL_SLAB = 256
VMEM_LIMIT_MB = 56


class Cfg(NamedTuple):
    d_model: int
    prompt_len: int
    sample_len: int
    n_sample: int
    depth: int
    hg: int
    b_heads: int
    q_lora: int
    kv_lora: int
    ff: int

    @property
    def m(self):
        return self.prompt_len + self.n_sample * self.sample_len

    @property
    def gw(self):
        return self.hg * A_HEAD_DIM

    @property
    def a_width(self):
        return len(DIL_GROUPS) * self.gw

    @property
    def ff_pad(self):
        unit = COL_SLAB if self.ff > COL_SLAB else LANES
        return -(-self.ff // unit) * unit

    @property
    def alpha(self):
        return (2.0 * self.depth) ** 0.25


class LayerWeight(NamedTuple):
    stack: jax.Array
    layer: int


def _tile(n, pref):
    if n <= pref:
        return n
    t = (pref // LANES) * LANES
    while n % t:
        t -= LANES
    return t


def _params(sem):
    return pltpu.CompilerParams(dimension_semantics=sem,
                                vmem_limit_bytes=VMEM_LIMIT_MB * 1024 * 1024)


def _matmul(a, bs, extras, out_defs, epilogue, *, tm, tn, name, n_col_blocks=None, b_block=None,
            scratch=(), col_split=1):
    m, k_dim = a.shape
    nj = bs[0].stack.shape[2] // tn if n_col_blocks is None else n_col_blocks
    b_block = (lambda j: j) if b_block is None else b_block
    nb, ne, no = len(bs), len(extras), len(out_defs)
    swap = lambda f: (lambda j, i: f(i, j))

    a_spec = pl.BlockSpec((tm, k_dim), lambda j, i: (i, 0))
    b_specs = [pl.BlockSpec((None, k_dim, tn), lambda j, i, layer=b.layer: (layer, 0, b_block(j)))
               for b in bs]
    in_specs = [a_spec] + b_specs + [pl.BlockSpec(bshape, swap(f)) for _, bshape, f in extras]
    out_specs = [pl.BlockSpec(d[2], swap(d[3] if len(d) > 3 else (lambda i, j: (i, j)))) for d in out_defs]
    out_shape = [jax.ShapeDtypeStruct(d[0], d[1]) for d in out_defs]

    def kernel(*refs):
        a_ref = refs[0]
        b_refs = refs[1:1 + nb]
        e_refs = refs[1 + nb:1 + nb + ne]
        o_refs = refs[1 + nb + ne:1 + nb + ne + no]
        s_refs = refs[1 + nb + ne + no:]
        width = tn // col_split
        for s in range(col_split):
            cs = slice(s * width, (s + 1) * width)
            accs = [jnp.dot(a_ref[...], b[:, cs], preferred_element_type=F32) for b in b_refs]
            epilogue(accs, e_refs, o_refs, pl.program_id(0), s_refs, cs)

    return pl.pallas_call(
        kernel, grid=(nj, m // tm), in_specs=in_specs, out_specs=out_specs, out_shape=out_shape,
        scratch_shapes=list(scratch), compiler_params=_params(("parallel", "arbitrary")), name=name,
    )(a, *[b.stack for b in bs], *[e[0] for e in extras])


def _rows_out(m, n, dtype, tm, tn):
    return ((m, n), dtype, (tm, tn))


def _rope_lanes(x, c, sl, sr, half):
    return x * c + pltpu.roll(x, LANES - half, 1) * sl + pltpu.roll(x, half, 1) * sr


def _row_tables(tabs, tm):
    return [(t, (tm, LANES), lambda i, j: (i, 0)) for t in tabs]


def _proj_qkv_group(x_bf, w_qkv, tabs_a, g, dil, cfg):
    m, gw = cfg.m, cfg.gw
    tm = _tile(m, 1024)
    half = A_ROT_DIM // 2
    ngroups = len(DIL_GROUPS)

    def epilogue(accs, e, o, j, scr, cs):
        acc = accs[0]
        h0 = cs.start // LANES
        local = [slice(h * LANES, (h + 1) * LANES) for h in range((cs.stop - cs.start) // LANES)]

        def emit(piece):
            if dil == 1:
                for h, hs in enumerate(local):
                    lo = (h0 + h) * LANES
                    o[0][:, lo:lo + LANES] = piece(hs).astype(BF16)
                return
            for h, hs in enumerate(local):
                scr[0][h0 + h] = piece(hs)
            for r in range(dil):
                for h in range(len(local)):
                    lo = r * gw + (h0 + h) * LANES
                    o[0][:, lo:lo + LANES] = scr[0][h0 + h, pl.ds(r, tm // dil, stride=dil), :].astype(BF16)

        rot = j < 2
        c = jnp.where(rot, e[0][...], 1.0)
        sl = jnp.where(rot, e[1][...], 0.0)
        sr = jnp.where(rot, e[2][...], 0.0)
        emit(lambda hs: _rope_lanes(acc[:, hs], c, sl, sr, half))

    out_def = ((m // dil, 3 * dil * gw), BF16, (tm // dil, dil * gw))
    scratch = [pltpu.VMEM((gw // LANES, tm, LANES), F32)] if dil > 1 else []
    return _matmul(x_bf, [w_qkv], _row_tables(tabs_a, tm), [out_def], epilogue, tm=tm, tn=gw,
                   n_col_blocks=3, b_block=lambda j: j * ngroups + g, scratch=scratch,
                   col_split=max(1, gw // COL_SLAB), name=f"proj_qkv_g{g}")[0]


def _proj_latent(x_bf, w_lat, q_norm, kv_norm, tabs_b, cfg):
    m, ql, kl = cfg.m, cfg.q_lora, cfg.kv_lora
    n = ql + kl + LANES
    tm = _tile(m, 512)
    half = ROPE_DIM // 2

    def rms(v, g):
        return v * lax.rsqrt(jnp.mean(v * v, axis=-1, keepdims=True) + RMS_EPS) * g

    def epilogue(accs, e, o, j, scr, cs):
        acc = accs[0]
        o[0][...] = rms(acc[:, :ql], e[0][...]).astype(BF16)
        o[1][...] = rms(acc[:, ql:ql + kl], e[1][...]).astype(BF16)
        o[2][...] = _rope_lanes(acc[:, ql + kl:], e[2][...], e[3][...], e[4][...], half).astype(BF16)

    extras = [(q_norm.reshape(1, ql), (1, ql), lambda i, j: (0, 0)),
              (kv_norm.reshape(1, kl), (1, kl), lambda i, j: (0, 0))] + _row_tables(tabs_b, tm)
    outs = [_rows_out(m, ql, BF16, tm, ql), _rows_out(m, kl, BF16, tm, kl),
            _rows_out(m, LANES, BF16, tm, LANES)]
    return _matmul(x_bf, [w_lat], extras, outs, epilogue, tm=tm, tn=n, name="proj_latent")


def _proj_gates(x_bf, w_g, cfg):
    tm, tn = _tile(cfg.m, 1024), _tile(2 * cfg.d_model, 1024)

    def epilogue(accs, e, o, j, scr, cs):
        o[0][:, cs] = jax.nn.sigmoid(accs[0]).astype(BF16)

    return _matmul(x_bf, [w_g], [], [_rows_out(cfg.m, 2 * cfg.d_model, BF16, tm, tn)], epilogue,
                   tm=tm, tn=tn, col_split=tn // COL_SLAB, name="proj_gates")[0]


def _proj_qt(c_q, w_uq_t, cos_t, sin_t, cfg):
    m, ql = c_q.shape
    n = w_uq_t.stack.shape[1]
    heads_per_tile = min(4, cfg.b_heads)
    tn = heads_per_tile * QK_DIM
    tm = _tile(m, 2048)
    half = ROPE_DIM // 2
    scale = QK_DIM ** -0.5 * LOG2_E

    def kernel(w_ref, c_ref, cos_ref, sin_ref, o_ref):
        acc = lax.dot_general(w_ref[...], c_ref[...], (((1,), (1,)), ((), ())),
                              preferred_element_type=F32) * scale
        cos, sin = cos_ref[...], sin_ref[...]
        for h in range(heads_per_tile):
            lo = h * QK_DIM
            o_ref[lo:lo + NOPE_DIM, :] = acc[lo:lo + NOPE_DIM, :].astype(BF16)
            x1 = acc[lo + NOPE_DIM:lo + NOPE_DIM + half, :]
            x2 = acc[lo + NOPE_DIM + half:lo + QK_DIM, :]
            o_ref[lo + NOPE_DIM:lo + NOPE_DIM + half, :] = (x1 * cos - x2 * sin).astype(BF16)
            o_ref[lo + NOPE_DIM + half:lo + QK_DIM, :] = (x1 * sin + x2 * cos).astype(BF16)

    tab = pl.BlockSpec((half, tm), lambda j, i: (0, i))
    return pl.pallas_call(
        kernel, grid=(n // tn, m // tm),
        in_specs=[pl.BlockSpec((None, tn, ql), lambda j, i: (w_uq_t.layer, j, 0)),
                  pl.BlockSpec((tm, ql), lambda j, i: (i, 0)), tab, tab],
        out_specs=pl.BlockSpec((tn, tm), lambda j, i: (j, i)),
        out_shape=jax.ShapeDtypeStruct((n, m), BF16),
        compiler_params=_params(("parallel", "arbitrary")), name="proj_qt",
    )(w_uq_t.stack, c_q, cos_t, sin_t)


def _proj_kv(c_kv, w_ukv, cfg):
    m, h = cfg.m, cfg.b_heads
    hw = NOPE_DIM + V_DIM
    tm, tn = _tile(m, 2048), _tile(h * hw, 1024)
    heads = tn // hw

    def epilogue(accs, e, o, j, scr, cs):
        acc = accs[0]
        for hh in range(heads):
            o[0][hh] = acc[:, hh * hw:hh * hw + NOPE_DIM].astype(BF16)
            o[1][hh] = acc[:, hh * hw + NOPE_DIM:(hh + 1) * hw].T.astype(BF16)

    k_def = ((h, m, NOPE_DIM), BF16, (heads, tm, NOPE_DIM), lambda i, j: (j, i, 0))
    vt_def = ((h, V_DIM, m), BF16, (heads, V_DIM, tm), lambda i, j: (j, 0, i))
    return _matmul(c_kv, [w_ukv], [], [k_def, vt_def], epilogue, tm=tm, tn=tn, name="proj_kv")


def _branch_a(out_a, w_ba, gates, cfg):
    tm, tn = _tile(cfg.m, 1024), _tile(cfg.d_model, 512)

    def epilogue(accs, e, o, j, scr, cs):
        o[0][:, cs] = e[0][:, cs].astype(F32) * accs[0]

    extras = [(gates, (tm, tn), lambda i, j: (i, j))]
    return _matmul(out_a, [w_ba], extras, [_rows_out(cfg.m, cfg.d_model, F32, tm, tn)], epilogue,
                   tm=tm, tn=tn, col_split=tn // COL_SLAB, name="branch_a")[0]


def _branch_b_merge(out_b, w_bb, gates, part_a, cfg):
    tm, tn = _tile(cfg.m, 512), _tile(cfg.d_model, 512)
    goff = cfg.d_model // tn

    def epilogue(accs, e, o, j, scr, cs):
        o[0][:, cs] = (e[1][:, cs] + e[0][:, cs].astype(F32) * accs[0]).astype(BF16)

    extras = [(gates, (tm, tn), lambda i, j: (i, j + goff)),
              (part_a, (tm, tn), lambda i, j: (i, j))]
    return _matmul(out_b, [w_bb], extras, [_rows_out(cfg.m, cfg.d_model, BF16, tm, tn)], epilogue,
                   tm=tm, tn=tn, col_split=tn // COL_SLAB, name="branch_b_merge")[0]


def _residual_matmul(a, w, x_res, cfg, *, tm, name):
    tm, tn = _tile(cfg.m, tm), _tile(cfg.d_model, 512)
    alpha = cfg.alpha

    def epilogue(accs, e, o, j, scr, cs):
        o[0][:, cs] = alpha * e[0][:, cs] + accs[0]

    extras = [(x_res, (tm, tn), lambda i, j: (i, j))]
    return _matmul(a, [w], extras, [_rows_out(cfg.m, cfg.d_model, F32, tm, tn)], epilogue,
                   tm=tm, tn=tn, col_split=tn // COL_SLAB, name=name)[0]


def _swiglu(x_bf, w_gate, w_up, cfg):
    tm, tn = _tile(cfg.m, 1024), _tile(cfg.ff_pad, 512)

    def epilogue(accs, e, o, j, scr, cs):
        g, u = accs
        o[0][:, cs] = (g * jax.nn.sigmoid(g) * u).astype(BF16)

    return _matmul(x_bf, [w_gate, w_up], [], [_rows_out(cfg.m, cfg.ff_pad, BF16, tm, tn)], epilogue,
                   tm=tm, tn=tn, col_split=tn // COL_SLAB, name="swiglu")[0]


def _layer_norm(y, g, b, cfg):
    m, d = y.shape
    tm = _tile(m, 256)

    def kernel(y_ref, g_ref, b_ref, o_ref, obf_ref):
        v = y_ref[...]
        mu = jnp.mean(v, axis=-1, keepdims=True)
        vc = v - mu
        var = jnp.mean(vc * vc, axis=-1, keepdims=True)
        out = vc * lax.rsqrt(var + LN_EPS) * g_ref[...] + b_ref[...]
        o_ref[...] = out
        obf_ref[...] = out.astype(BF16)

    row = pl.BlockSpec((tm, d), lambda i: (i, 0))
    vec = pl.BlockSpec((1, d), lambda i: (0, 0))
    return pl.pallas_call(
        kernel, grid=(m // tm,), in_specs=[row, vec, vec], out_specs=[row, row],
        out_shape=[jax.ShapeDtypeStruct((m, d), F32), jax.ShapeDtypeStruct((m, d), BF16)],
        compiler_params=_params(("parallel",)), name="layer_norm",
    )(y, g.reshape(1, d), b.reshape(1, d))


def _final_layer_norm(y, g, b, cfg):
    m, d = y.shape
    tm = _tile(cfg.sample_len, 256)
    p_blocks = cfg.prompt_len // tm

    def kernel(y_ref, g_ref, b_ref, op_ref, os_ref):
        v = y_ref[...]
        mu = jnp.mean(v, axis=-1, keepdims=True)
        vc = v - mu
        var = jnp.mean(vc * vc, axis=-1, keepdims=True)
        out = vc * lax.rsqrt(var + LN_EPS) * g_ref[...] + b_ref[...]
        i = pl.program_id(0)

        @pl.when(i < p_blocks)
        def _():
            op_ref[...] = out

        @pl.when(i >= p_blocks)
        def _():
            os_ref[...] = out

    row = pl.BlockSpec((tm, d), lambda i: (i, 0))
    vec = pl.BlockSpec((1, d), lambda i: (0, 0))
    p_spec = pl.BlockSpec((tm, d), lambda i: (jnp.minimum(i, p_blocks - 1), 0))
    s_spec = pl.BlockSpec((tm, d), lambda i: (jnp.maximum(i - p_blocks, 0), 0))
    return pl.pallas_call(
        kernel, grid=(m // tm,), in_specs=[row, vec, vec], out_specs=[p_spec, s_spec],
        out_shape=[jax.ShapeDtypeStruct((cfg.prompt_len, d), F32),
                   jax.ShapeDtypeStruct((m - cfg.prompt_len, d), F32)],
        compiler_params=_params(("arbitrary",)), name="final_layer_norm",
    )(y, g.reshape(1, d), b.reshape(1, d))


def _segment(row0, p_len, s_len):
    in_prompt = row0 < p_len
    s_idx = jnp.maximum(row0 - p_len, 0) // s_len
    lo = jnp.where(in_prompt, 0, p_len + s_idx * s_len)
    return lo, jnp.where(in_prompt, p_len, s_len)


def _banded_attention(qkv_g, g, dil, cfg):
    hg, gw = cfg.hg, cfg.gw
    rows = cfg.m // dil
    p_len, s_len = cfg.prompt_len // dil, cfg.sample_len // dil
    bq = 2 * BAND_W
    kw = bq + 2 * BAND_W
    rc = _tile(rows, 2048)
    nsub = rc // bq
    scale = A_HEAD_DIM ** -0.5

    def kernel(q_ref, k_ref, v_ref, o_ref, lse_ref):
        base = pl.program_id(2) * rc

        def body(t, carry):
            s0 = pl.multiple_of(t * bq, bq)
            g0 = base + s0
            ks = pl.multiple_of(jnp.clip(g0 - BAND_W, 0, rows - kw), BAND_W)
            q = q_ref[pl.ds(s0, bq), :]
            k = k_ref[pl.ds(ks, kw), :]
            v = v_ref[pl.ds(ks, kw), :]
            s = lax.dot_general(q, k, (((1,), (1,)), ((), ())), preferred_element_type=F32) * scale
            qrow = g0 + lax.broadcasted_iota(jnp.int32, (bq, kw), 0)
            krow = ks + lax.broadcasted_iota(jnp.int32, (bq, kw), 1)
            lo, n = _segment(g0, p_len, s_len)
            valid = (jnp.abs(krow - qrow) <= BAND_W) & (krow >= lo) & (krow < lo + n)
            s = jnp.where(valid, s, NEG_BIG)
            mx = jnp.max(s, axis=-1, keepdims=True)
            p = jnp.exp(s - mx)
            l = jnp.sum(p, axis=-1, keepdims=True)
            pv = jnp.dot(p.astype(BF16), v, preferred_element_type=F32)
            o_ref[pl.ds(s0, bq), :] = pv / l
            lse_ref[pl.ds(s0, bq), :] = jnp.broadcast_to(mx + jnp.log(l), (bq, LANES))
            return carry

        lax.fori_loop(0, nsub, body, 0, unroll=BAND_UNROLL)

    def col(kind):
        return lambda r, j, c: (0, (kind * dil + r) * hg + j)

    q_spec = pl.BlockSpec((rc, LANES), lambda r, j, c: (c, r * hg + j))
    k_spec = pl.BlockSpec((rows, LANES), col(1))
    v_spec = pl.BlockSpec((rows, LANES), col(2))
    o_spec = pl.BlockSpec((rc, LANES), lambda r, j, c: (c, r * hg + j))
    shp = jax.ShapeDtypeStruct((rows, dil * gw), F32)
    return pl.pallas_call(
        kernel, grid=(dil, hg, rows // rc), in_specs=[q_spec, k_spec, v_spec],
        out_specs=[o_spec, o_spec], out_shape=[shp, shp],
        compiler_params=_params(("parallel", "parallel", "arbitrary")), name=f"banded_attention_g{g}",
    )(qkv_g, qkv_g, qkv_g)


def _mix_groups(outs, lses, cfg):
    m, gw = cfg.m, cfg.gw
    tm = _tile(m, 256)
    ng = len(outs)
    dils = [d for _, d in DIL_GROUPS]

    def kernel(*refs):
        o_refs, l_refs, out_ref = refs[:ng], refs[ng:2 * ng], refs[2 * ng]
        scr = list(refs[2 * ng + 1:])

        def token_major(ref, d):
            if d == 1:
                return ref[...]
            buf = scr.pop()
            for r in range(d):
                for h in range(cfg.hg):
                    lo = r * gw + h * LANES
                    buf[h, pl.ds(r, tm // d, stride=d), :] = ref[:, lo:lo + LANES]
            return jnp.concatenate([buf[h] for h in range(cfg.hg)], axis=1)

        os_ = [token_major(r, d) for r, d in zip(o_refs, dils)]
        ls = [token_major(r, d) for r, d in zip(l_refs, dils)]
        mx = functools.reduce(jnp.maximum, ls)
        es = [jnp.exp(l - mx) for l in ls]
        inv = 1.0 / functools.reduce(jnp.add, es)
        for gi in range(ng):
            out_ref[:, gi * gw:(gi + 1) * gw] = (os_[gi] * (es[gi] * inv)).astype(BF16)

    specs = [pl.BlockSpec((tm // d, d * gw), lambda i: (i, 0)) for d in dils]
    n_scr = 2 * sum(d > 1 for d in dils)
    return pl.pallas_call(
        kernel, grid=(m // tm,), in_specs=specs * 2,
        out_specs=pl.BlockSpec((tm, ng * gw), lambda i: (i, 0)),
        out_shape=jax.ShapeDtypeStruct((m, ng * gw), BF16),
        scratch_shapes=[pltpu.VMEM((cfg.hg, tm, LANES), F32)] * n_scr,
        compiler_params=_params(("parallel",)), name="mix_groups",
    )(*outs, *lses)


def _latent_attention(qt, kn, kpe, vt, cfg):
    m, h = cfg.m, cfg.b_heads
    p_len, s_len = cfg.prompt_len, cfg.sample_len
    tq = _tile(s_len, 512)
    tk = _tile(s_len // 4, 512)
    assert s_len % (4 * tk) == 0 and p_len % (4 * tk) == 0, "each sequence needs >= 2 chunk pairs"

    nq = 4
    assert s_len % (nq * tq) == 0 and p_len % (nq * tq) == 0
    assert m % tk == 0

    def kernel(q_ref, kn_ref, kpe_ref, vt_ref, o_ref, *scratch_refs):
        row0 = pl.program_id(1) * (nq * tq)
        lo, n = _segment(row0, p_len, s_len)
        npair = n // (2 * tk)
        streams = [scratch_refs[5 * u:5 * u + 5] for u in range(nq)]
        qts = [q_ref[:, u * tq:(u + 1) * tq] for u in range(nq)]

        def scores(u, c, s_ref):
            ks = pl.multiple_of(lo + c * tk, tk)
            k = jnp.concatenate([kn_ref[pl.ds(ks, tk), :], kpe_ref[pl.ds(ks, tk), :][:, :ROPE_DIM]],
                                axis=1)
            s = jnp.dot(k, qts[u], preferred_element_type=F32)
            s_ref[...] = s
            return jnp.max(s, axis=0, keepdims=True)

        def softmax(s_ref, p_ref, mx, chunk_max):
            mx_new = jnp.maximum(mx, chunk_max)
            p_ref[...] = jnp.exp2((s_ref[...] - mx_new).astype(BF16))
            return mx_new, jnp.exp2(mx - mx_new)

        ones_rows = jnp.ones((SUM_ROWS, tk), BF16)

        def values(c, p_ref, acc_ref, corr):
            ks = pl.multiple_of(lo + c * tk, tk)
            lhs = jnp.concatenate([vt_ref[:, pl.ds(ks, tk)], ones_rows], axis=0)
            acc_ref[...] = corr * acc_ref[...] + jnp.dot(lhs, p_ref[...], preferred_element_type=F32)

        def pair(i, carry, first=False, last=False):
            c = 2 * i
            s_a, s_b, p_a, p_b, acc = zip(*streams)
            mx, max_a, corr_prev = (list(t) for t in zip(*carry))
            corr_a, corr_b, max_b = [None] * nq, [None] * nq, [None] * nq
            groups = [range(g0, min(g0 + STREAM_GROUP, nq)) for g0 in range(0, nq, STREAM_GROUP)]
            for us in groups:
                for u in us:
                    max_b[u] = scores(u, c + 1, s_b[u])
                for u in us:
                    mx[u], corr_a[u] = softmax(s_a[u], p_a[u], mx[u], max_a[u])
                    if not first:
                        values(c - 1, p_b[u], acc[u], corr_prev[u])
            for us in groups:
                if not last:
                    for u in us:
                        max_a[u] = scores(u, c + 2, s_a[u])
                for u in us:
                    values(c, p_a[u], acc[u], corr_a[u])
                    mx[u], corr_b[u] = softmax(s_b[u], p_b[u], mx[u], max_b[u])
            return tuple(zip(mx, max_a, corr_b))

        row = jnp.zeros((1, tq), F32)
        init = []
        for u, (s_a, _, _, _, acc_ref) in enumerate(streams):
            acc_ref[...] = jnp.zeros_like(acc_ref)
            init.append((row - jnp.inf, scores(u, 0, s_a), row))
        carry = pair(0, tuple(init), first=True)
        carry = lax.fori_loop(1, npair - 1, pair, carry)
        carry = pair(npair - 1, carry, last=True)
        for u, (_, _, _, p_b, acc_ref) in enumerate(streams):
            values(2 * npair - 1, p_b, acc_ref, carry[u][2])
            acc = acc_ref[...]
            o_ref[u * tq:(u + 1) * tq, :] = (acc[:V_DIM] / acc[V_DIM:V_DIM + 1]).T.astype(BF16)

    scratch = [pltpu.VMEM((tk, tq), F32), pltpu.VMEM((tk, tq), F32), pltpu.VMEM((tk, tq), BF16),
               pltpu.VMEM((tk, tq), BF16), pltpu.VMEM((V_DIM + SUM_ROWS, tq), F32)] * nq
    q_spec = pl.BlockSpec((QK_DIM, nq * tq), lambda hh, i: (hh, i))
    kn_spec = pl.BlockSpec((None, m, NOPE_DIM), lambda hh, i: (hh, 0, 0))
    kpe_spec = pl.BlockSpec((m, LANES), lambda hh, i: (0, 0))
    vt_spec = pl.BlockSpec((None, V_DIM, m), lambda hh, i: (hh, 0, 0))
    o_spec = pl.BlockSpec((nq * tq, V_DIM), lambda hh, i: (i, hh))
    return pl.pallas_call(
        kernel, grid=(h, m // (nq * tq)), in_specs=[q_spec, kn_spec, kpe_spec, vt_spec], out_specs=o_spec,
        out_shape=jax.ShapeDtypeStruct((m, h * V_DIM), BF16), scratch_shapes=scratch,
        compiler_params=_params(("parallel", "arbitrary")), name="latent_attention",
    )(qt, kn, kpe, vt)


def _rope_angles(pos, rot_dim):
    half = rot_dim // 2
    inv_freq = ROPE_THETA ** (-jnp.arange(half, dtype=F32) / half)
    return pos[:, None] * inv_freq[None, :]


def _rope_tables(pos, rot_dim):
    ang = _rope_angles(pos, rot_dim)
    cos, sin = jnp.cos(ang), jnp.sin(ang)
    rest = LANES - rot_dim
    ones = jnp.ones((pos.shape[0], rest), F32)
    zeros = jnp.zeros((pos.shape[0], rest), F32)
    zh = jnp.zeros_like(sin)
    c = jnp.concatenate([cos, cos, ones], axis=1)
    sl = jnp.concatenate([-sin, zh, zeros], axis=1)
    sr = jnp.concatenate([zh, sin, zeros], axis=1)
    return c, sl, sr


def _prep_weights(cfg, w_in, w_uq, w_ukv, w_ba, w_bb, w_o, w_gate, w_up, w_down):
    aw, ql, kl = cfg.a_width, cfg.q_lora, cfg.kv_lora
    off_cq = 3 * aw
    off_ga = off_cq + ql + kl + ROPE_DIM
    fpad = cfg.ff_pad - cfg.ff
    return dict(
        w_qkv=w_in[:, :, :off_cq].astype(BF16),
        w_lat=jnp.pad(w_in[:, :, off_cq:off_ga], ((0, 0), (0, 0), (0, LANES - ROPE_DIM))).astype(BF16),
        w_g=w_in[:, :, off_ga:].astype(BF16),
        w_uq_t=jnp.swapaxes(w_uq, 1, 2).astype(BF16), w_ukv=w_ukv.astype(BF16),
        w_ba=w_ba.astype(BF16), w_bb=w_bb.astype(BF16), w_o=w_o.astype(BF16),
        w_gate=jnp.pad(w_gate, ((0, 0), (0, 0), (0, fpad))).astype(BF16),
        w_up=jnp.pad(w_up, ((0, 0), (0, 0), (0, fpad))).astype(BF16),
        w_down=jnp.pad(w_down, ((0, 0), (0, fpad), (0, 0))).astype(BF16),
    )


def _layer(x, x_bf, w, q_norm, kv_norm, ln1_g, ln1_b, ln2_g, ln2_b, tabs_a, tabs_b, tabs_bt, cfg, last):
    c_q, c_kv, kpe = _proj_latent(x_bf, w["w_lat"], q_norm, kv_norm, tabs_b, cfg)
    gates = _proj_gates(x_bf, w["w_g"], cfg)

    outs, lses = [], []
    for g, (_, dil) in enumerate(DIL_GROUPS):
        qkv_g = _proj_qkv_group(x_bf, w["w_qkv"], tabs_a, g, dil, cfg)
        o, s = _banded_attention(qkv_g, g, dil, cfg)
        outs.append(o)
        lses.append(s)
    out_a = _mix_groups(outs, lses, cfg)

    qt = _proj_qt(c_q, w["w_uq_t"], *tabs_bt, cfg)
    kn, vt = _proj_kv(c_kv, w["w_ukv"], cfg)
    out_b = _latent_attention(qt, kn, kpe, vt, cfg)

    part_a = _branch_a(out_a, w["w_ba"], gates, cfg)
    merged = _branch_b_merge(out_b, w["w_bb"], gates, part_a, cfg)
    y1 = _residual_matmul(merged, w["w_o"], x, cfg, tm=1024, name="out_proj")
    x1, x1_bf = _layer_norm(y1, ln1_g, ln1_b, cfg)

    hmid = _swiglu(x1_bf, w["w_gate"], w["w_up"], cfg)
    y2 = _residual_matmul(hmid, w["w_down"], x1, cfg, tm=256, name="ffn_down")
    return (_final_layer_norm if last else _layer_norm)(y2, ln2_g, ln2_b, cfg)


def _trunk(cfg, x_prompt, x_sample, w_in, mla_q_norm, w_uq, mla_kv_norm, w_ukv, w_branch_a, w_branch_b,
           w_out, ln1_g, ln1_b, w_ffn_gate, w_ffn_up, w_ffn_down, ln2_g, ln2_b):
    d = cfg.d_model
    x = jnp.concatenate([x_prompt.reshape(-1, d), x_sample.reshape(-1, d)], axis=0)
    x_bf = x.astype(BF16)
    pos = jnp.concatenate([jnp.arange(cfg.prompt_len, dtype=F32),
                           jnp.tile(jnp.arange(cfg.sample_len, dtype=F32), cfg.n_sample)])
    tabs_a = _rope_tables(pos, A_ROT_DIM)
    tabs_b = _rope_tables(pos, ROPE_DIM)
    ang_t = _rope_angles(pos, ROPE_DIM).T
    tabs_bt = (jnp.cos(ang_t), jnp.sin(ang_t))
    stacks = _prep_weights(cfg, w_in, w_uq, w_ukv, w_branch_a, w_branch_b, w_out, w_ffn_gate, w_ffn_up,
                           w_ffn_down)
    for l in range(cfg.depth):
        w = {name: LayerWeight(stack, l) for name, stack in stacks.items()}
        last = l == cfg.depth - 1
        out = _layer(x, x_bf, w, mla_q_norm[l], mla_kv_norm[l], ln1_g[l], ln1_b[l], ln2_g[l],
                     ln2_b[l], tabs_a, tabs_b, tabs_bt, cfg, last)
        if not last:
            x, x_bf = out
    y_prompt, y_sample = out
    return (y_prompt.reshape(x_prompt.shape), y_sample.reshape(x_sample.shape))


def kernel(x_prompt, x_sample, w_in, mla_q_norm, w_uq, mla_kv_norm, w_ukv, w_branch_a, w_branch_b, w_out,
           ln1_g, ln1_b, w_ffn_gate, w_ffn_up, w_ffn_down, ln2_g, ln2_b):
    assert x_prompt.shape[0] == 1, "one prompt sequence"
    cfg = Cfg(d_model=x_prompt.shape[-1], prompt_len=x_prompt.shape[1], sample_len=x_sample.shape[1],
              n_sample=x_sample.shape[0], depth=w_in.shape[0],
              hg=w_branch_a.shape[1] // (len(DIL_GROUPS) * A_HEAD_DIM),
              b_heads=w_branch_b.shape[1] // V_DIM, q_lora=w_uq.shape[1], kv_lora=w_ukv.shape[1],
              ff=w_ffn_gate.shape[2])
    return _trunk(cfg, x_prompt, x_sample, w_in, mla_q_norm, w_uq, mla_kv_norm, w_ukv, w_branch_a,
                  w_branch_b, w_out, ln1_g, ln1_b, w_ffn_gate, w_ffn_up, w_ffn_down, ln2_g, ln2_b)
```
